```python
import math
import jax, jax.numpy as jnp
from jax import lax
import numpy as np


D_MODEL = 4096
BATCH = 4
SEQ = 4096
DEPTH = 2

GRID_W = 64
CTX_LEN = 256
MIX_HALF = D_MODEL // 2

GDN_HEADS = 16
GDN_DK = 128
GDN_DV = 128
GDN_QK = GDN_HEADS * GDN_DK
GDN_V = GDN_HEADS * GDN_DV
GDN_CONV = 3
GDN_CHUNK = 64

DIFF_HEADS = 8
DIFF_D = 128
DIFF_QK = DIFF_HEADS * 2 * DIFF_D
DIFF_V = DIFF_HEADS * 2 * DIFF_D
Q_BLOCK = 128
ROPE_BASE = 10000.0

CONF_CH = MIX_HALF
CONF_K = 31
SC_CH = MIX_HALF
SC_K = 3

D_FF = -(-(8 * D_MODEL) // (3 * 256)) * 256

EVEN_SPLIT = (2 * GDN_QK + GDN_V, GDN_V, 4 * GDN_HEADS, DIFF_QK, DIFF_QK, DIFF_V)
EVEN_IN = sum(EVEN_SPLIT)
EVEN_MIX = GDN_V + DIFF_V
ODD_SPLIT = (CONF_CH, CONF_CH, SC_CH, SC_CH, SC_CH)
ODD_IN = sum(ODD_SPLIT)
ODD_MIX = CONF_CH + SC_CH

F32 = jnp.float32

kernel_name = 'hybrid_gdn_diffattn_conformer_shortconv_prefix_dit'


def _split(p, sizes):
    idx = np.cumsum(sizes)[:-1].tolist()
    return jnp.split(p, idx, axis=-1)


def proj(h, w):
    return jnp.einsum('btd,de->bte', h, w)


def rmsnorm(x, g, eps=1e-6):
    xf = x.astype(F32)
    y = xf * lax.rsqrt(jnp.mean(xf * xf, axis=-1, keepdims=True) + eps)
    return (y * g.astype(F32)).astype(x.dtype)


def layernorm(x, g, b, eps=1e-5):
    xf = x.astype(F32)
    mu = jnp.mean(xf, axis=-1, keepdims=True)
    xc = xf - mu
    y = xc * lax.rsqrt(jnp.mean(xc * xc, axis=-1, keepdims=True) + eps)
    return (y * g.astype(F32) + b.astype(F32)).astype(x.dtype)


def l2norm(x, eps=1e-6):
    return x * lax.rsqrt(jnp.sum(x * x, axis=-1, keepdims=True) + eps)


def modulate(x, shift, scale):
    return x * (1 + scale) + shift


def adaln(cond, w, b):
    m = jnp.einsum('...d,de->...e', jax.nn.silu(cond), w) + b
    return jnp.split(m, 6, axis=-1)


def dwconv(x, w):
    k_w, ch = w.shape
    pad = (k_w - 1) // 2
    return lax.conv_general_dilated(
        x, w.astype(x.dtype)[:, None, :], window_strides=(1,),
        padding=[(pad, k_w - 1 - pad)], dimension_numbers=('NWC', 'WIO', 'NWC'),
        feature_group_count=ch)


def axial_rope_tables(rows, dim):
    half = dim // 2
    inv = 1.0 / (ROPE_BASE ** (jnp.arange(0, half, 2, dtype=F32) / half))
    r = jnp.repeat(jnp.arange(rows), GRID_W).astype(F32)[:, None]
    col = jnp.tile(jnp.arange(GRID_W), rows).astype(F32)[:, None]
    ang = jnp.concatenate([r * inv, r * inv, col * inv, col * inv], axis=-1)
    return jnp.cos(ang), jnp.sin(ang)


def apply_axial_rope(x, cos, sin):
    half = x.shape[-1] // 2
    qt = half // 2
    xf = x.astype(F32)
    rot = lambda t: jnp.concatenate([-t[..., qt:], t[..., :qt]], axis=-1)
    xr = jnp.concatenate([rot(xf[..., :half]), rot(xf[..., half:])], axis=-1)
    return (xf * cos + xr * sin).astype(x.dtype)


def gated_delta_chunked(q, k, v, g, beta, s0):
    bsz, nh, t_len, dk = q.shape
    dv = v.shape[-1]
    n = t_len // GDN_CHUNK
    ch = lambda a: a.reshape(bsz, nh, n, GDN_CHUNK, *a.shape[3:])
    q = ch(q * dk ** -0.5)
    k = ch(k)
    v = ch(v)
    beta = ch(beta)
    g = jnp.cumsum(ch(g), axis=-1)
    incl = jnp.tril(jnp.ones((GDN_CHUNK, GDN_CHUNK), bool))
    strict = jnp.tril(jnp.ones((GDN_CHUNK, GDN_CHUNK), bool), -1)
    gdiff = g[..., :, None] - g[..., None, :]
    decay = jnp.where(incl, jnp.exp(jnp.where(incl, gdiff, 0.0)), 0.0)
    kb = k * beta[..., None]
    lmat = jnp.where(strict, jnp.einsum('bhncd,bhnsd->bhncs', kb, k) * decay, 0.0)
    amat = lmat + jnp.eye(GDN_CHUNK, dtype=F32)
    rhs = jnp.concatenate([v * beta[..., None], kb * jnp.exp(g)[..., None]], axis=-1)
    sol = lax.linalg.triangular_solve(amat, rhs, left_side=True, lower=True, unit_diagonal=True)
    u, w = sol[..., :dv], sol[..., dv:]
    qk = jnp.where(incl, jnp.einsum('bhncd,bhnsd->bhncs', q, k) * decay, 0.0)
    q_dec = q * jnp.exp(g)[..., None]
    g_last = g[..., -1]
    k_tail = k * jnp.exp(g_last[..., None] - g)[..., None]
    front = lambda a: jnp.moveaxis(a, 2, 0)

    def step(s, xs):
        qd_c, qk_c, u_c, w_c, kt_c, gl_c = xs
        v_new = u_c - jnp.einsum('bhck,bhkv->bhcv', w_c, s)
        o_c = jnp.einsum('bhck,bhkv->bhcv', qd_c, s) + jnp.einsum('bhcs,bhsv->bhcv', qk_c, v_new)
        s = s * jnp.exp(gl_c)[..., None, None] + jnp.einsum('bhck,bhcv->bhkv', kt_c, v_new)
        return s, o_c

    xs = (front(q_dec), front(qk), front(u), front(w), front(k_tail), front(g_last))
    s_fin, o = lax.scan(step, s0, xs)
    o = jnp.moveaxis(o, 0, 2).reshape(bsz, nh, t_len, dv)
    return o, s_fin


def gdn_inputs(qkv, ab, conv_w, a_log, dt_bias):
    bsz, t_len, _ = qkv.shape
    qkv = jax.nn.silu(dwconv(qkv, conv_w))
    q, k, v = _split(qkv, (GDN_QK, GDN_QK, GDN_V))
    heads = lambda t, d: t.reshape(bsz, t_len, GDN_HEADS, d).transpose(0, 2, 1, 3).astype(F32)
    q = l2norm(heads(q, GDN_DK))
    k = l2norm(heads(k, GDN_DK))
    v = heads(v, GDN_DV)
    ab = ab.reshape(bsz, t_len, 4, GDN_HEADS).astype(F32)
    g = -jnp.exp(a_log.astype(F32)) * jax.nn.softplus(ab[:, :, :2] + dt_bias.astype(F32))
    beta = jax.nn.sigmoid(ab[:, :, 2:])
    return q, k, v, g.transpose(2, 0, 3, 1), beta.transpose(2, 0, 3, 1)


def gdn_bidir(q, k, v, g, beta, s0_f, s0_b):
    o_f, s_f = gated_delta_chunked(q, k, v, g[0], beta[0], s0_f)
    fl = lambda t: jnp.flip(t, axis=2)
    o_b, s_b = gated_delta_chunked(fl(q), fl(k), fl(v), jnp.flip(g[1], -1), jnp.flip(beta[1], -1), s0_b)
    return o_f + fl(o_b), s_f, s_b


def gdn_out(o, z, gdn_norm):
    bsz, nh, t_len, dv = o.shape
    o = o.transpose(0, 2, 1, 3)
    y = rmsnorm(o, gdn_norm) * jax.nn.silu(z.reshape(bsz, t_len, nh, dv).astype(F32))
    return y.reshape(bsz, t_len, nh * dv).astype(z.dtype)


def diff_heads(dq, dk, dv):
    bsz, t_len, _ = dq.shape
    qk = lambda t: t.reshape(bsz, t_len, DIFF_HEADS, 2, DIFF_D).transpose(0, 2, 3, 1, 4)
    v = dv.reshape(bsz, t_len, DIFF_HEADS, 2 * DIFF_D).transpose(0, 2, 1, 3)
    return qk(dq) * DIFF_D ** -0.5, qk(dk), v


def diff_attention(q, k, v, lam):
    bsz, nh, _, t_len, d = q.shape
    nb = t_len // Q_BLOCK
    qb = jnp.moveaxis(q.reshape(bsz, nh, 2, nb, Q_BLOCK, d), 3, 0)
    vf = v.astype(F32)

    def block(qi):
        s = jnp.einsum('bhmqd,bhmkd->bhmqk', qi, k, preferred_element_type=F32)
        p = jax.nn.softmax(s, axis=-1)
        a = p[:, :, 0] - lam * p[:, :, 1]
        return jnp.einsum('bhqk,bhke->bhqe', a, vf)

    o = lax.map(block, qb)
    return jnp.moveaxis(o, 0, 2).reshape(bsz, nh, t_len, 2 * d)


def diff_out(o, diff_norm, lambda_init, dtype):
    bsz, nh, t_len, e = o.shape
    y = rmsnorm(o, diff_norm, eps=1e-5) * (1.0 - lambda_init)
    return y.transpose(0, 2, 1, 3).reshape(bsz, t_len, nh * e).astype(dtype)


def mixer_ab(xn, cn, cos, sin, w_in, w_out, conv_w, a_log, dt_bias, gdn_norm, lam_vecs, diff_norm,
             lambda_init, need_ctx):
    px = proj(xn, w_in)
    pc = proj(cn, w_in)
    qkv_x, z_x, ab_x, dq_x, dk_x, dv_x = _split(px, EVEN_SPLIT)
    qkv_c, z_c, ab_c, dq_c, dk_c, dv_c = _split(pc, EVEN_SPLIT)
    qa_c, ka_c, va_c, g_c, b_c = gdn_inputs(qkv_c, ab_c, conv_w, a_log, dt_bias)
    qa_x, ka_x, va_x, g_x, b_x = gdn_inputs(qkv_x, ab_x, conv_w, a_log, dt_bias)
    s0 = jnp.zeros((cn.shape[0], GDN_HEADS, GDN_DK, GDN_DV), F32)
    oa_c, s_f, s_b = gdn_bidir(qa_c, ka_c, va_c, g_c, b_c, s0, s0)
    oa_x, _, _ = gdn_bidir(qa_x, ka_x, va_x, g_x, b_x, s_f, s_b)
    lv = lam_vecs.astype(F32)
    lam = jnp.exp(jnp.sum(lv[0] * lv[1])) - jnp.exp(jnp.sum(lv[2] * lv[3])) + lambda_init
    qb_c, kb_c, vb_c = diff_heads(dq_c, dk_c, dv_c)
    qb_x, kb_x, vb_x = diff_heads(dq_x, dk_x, dv_x)
    qb_x = apply_axial_rope(qb_x, cos, sin)
    kb_x = apply_axial_rope(kb_x, cos, sin)
    k_all = jnp.concatenate([kb_x, kb_c], axis=3)
    v_all = jnp.concatenate([vb_x, vb_c], axis=2)
    ob_x = diff_attention(qb_x, k_all, v_all, lam)
    y_x = proj(jnp.concatenate([gdn_out(oa_x, z_x, gdn_norm),
                                diff_out(ob_x, diff_norm, lambda_init, xn.dtype)], axis=-1), w_out)
    y_c = None
    if need_ctx:
        ob_c = diff_attention(qb_c, kb_c, vb_c, lam)
        y_c = proj(jnp.concatenate([gdn_out(oa_c, z_c, gdn_norm),
                                    diff_out(ob_c, diff_norm, lambda_init, cn.dtype)], axis=-1), w_out)
    return y_x, y_c


def mixer_cd(h, w_in, w_out, conf_dw, conf_dw_b, conf_ln_g, conf_ln_b, sc_conv):
    p = proj(h, w_in)
    glu_a, glu_b, gate_b, gate_c, sh = _split(p, ODD_SPLIT)
    yc = glu_a * jax.nn.sigmoid(glu_b)
    yc = dwconv(yc, conf_dw) + conf_dw_b
    yc = jax.nn.silu(layernorm(yc, conf_ln_g, conf_ln_b))
    yd = gate_b * dwconv(gate_c * sh, sc_conv)
    return proj(jnp.concatenate([yc, yd], axis=-1), w_out)


def swiglu(h, wg, wu, wd):
    return proj(jax.nn.silu(proj(h, wg)) * proj(h, wu), wd)


def setup_inputs(seed: int = 0) -> dict:
    key = jax.random.key(seed)
    ks = iter(jax.random.split(key, 32))
    nrm = lambda shape, scale: jax.random.normal(next(ks), shape, F32) * scale
    D = D_MODEL
    ne = (DEPTH + 1) // 2
    no = DEPTH // 2
    inp = {}
    inp['x'] = nrm((BATCH, SEQ, D), 1.0)
    inp['c'] = nrm((BATCH, D), 1.0)
    inp['ctx'] = nrm((BATCH, CTX_LEN, D), 1.0)
    inp['c_ctx'] = nrm((D,), 1.0)
    inp['w_ada'] = nrm((DEPTH, D, 6 * D), 0.5 * D ** -0.5)
    inp['b_ada'] = nrm((DEPTH, 6 * D), 0.02)
    inp['g_mix_pre'] = 1.0 + nrm((DEPTH, D), 0.05)
    inp['g_mix_post'] = 1.0 + nrm((DEPTH, D), 0.05)
    inp['g_ffn_pre'] = 1.0 + nrm((DEPTH, D), 0.05)
    inp['g_ffn_post'] = 1.0 + nrm((DEPTH, D), 0.05)
    inp['w_ff_gate'] = nrm((DEPTH, D, D_FF), D ** -0.5)
    inp['w_ff_up'] = nrm((DEPTH, D, D_FF), D ** -0.5)
    inp['w_ff_down'] = nrm((DEPTH, D_FF, D), D_FF ** -0.5)
    inp['w_in_even'] = nrm((ne, D, EVEN_IN), D ** -0.5)
    inp['w_out_even'] = nrm((ne, EVEN_MIX, D), EVEN_MIX ** -0.5)
    inp['gdn_conv'] = nrm((ne, GDN_CONV, 2 * GDN_QK + GDN_V), GDN_CONV ** -0.5)
    inp['gdn_a_log'] = jnp.log(jax.random.uniform(next(ks), (ne, 2, GDN_HEADS), F32, 1.0, 16.0))
    dt = jnp.exp(jax.random.uniform(next(ks), (ne, 2, GDN_HEADS), F32, math.log(1e-3), math.log(1e-1)))
    inp['gdn_dt_bias'] = dt + jnp.log(-jnp.expm1(-dt))
    inp['gdn_norm'] = 1.0 + nrm((ne, GDN_DV), 0.05)
    inp['diff_lambda'] = nrm((ne, 4, DIFF_D), 0.1)
    inp['diff_norm'] = 1.0 + nrm((ne, 2 * DIFF_D), 0.05)
    inp['w_in_odd'] = nrm((no, D, ODD_IN), D ** -0.5)
    inp['w_out_odd'] = nrm((no, ODD_MIX, D), ODD_MIX ** -0.5)
    inp['conf_dw'] = nrm((no, CONF_K, CONF_CH), CONF_K ** -0.5)
    inp['conf_dw_b'] = nrm((no, CONF_CH), 0.02)
    inp['conf_ln_g'] = 1.0 + nrm((no, CONF_CH), 0.05)
    inp['conf_ln_b'] = nrm((no, CONF_CH), 0.02)
    inp['sc_conv'] = nrm((no, SC_K, SC_CH), SC_K ** -0.5)
    return inp


def reference(x, c, ctx, c_ctx, w_ada, b_ada, g_mix_pre, g_mix_post, g_ffn_pre, g_ffn_post,
              w_ff_gate, w_ff_up, w_ff_down, w_in_even, w_out_even, gdn_conv, gdn_a_log,
              gdn_dt_bias, gdn_norm, diff_lambda, diff_norm, w_in_odd, w_out_odd, conf_dw,
              conf_dw_b, conf_ln_g, conf_ln_b, sc_conv):
    n_lat = x.shape[1]
    rows = n_lat // GRID_W
    cos, sin = axial_rope_tables(rows, DIFF_D)
    h = ctx
    for i in range(DEPTH):
        last = i == DEPTH - 1
        j = i // 2
        even = i % 2 == 0
        mx = [m[:, None, :] for m in adaln(c, w_ada[i], b_ada[i])]
        mc = adaln(c_ctx, w_ada[i], b_ada[i])
        xn = modulate(rmsnorm(x, g_mix_pre[i]), mx[0], mx[1])
        y_c = None
        if even:
            cn = modulate(rmsnorm(h, g_mix_pre[i]), mc[0], mc[1])
            y_x, y_c = mixer_ab(xn, cn, cos, sin, w_in_even[j], w_out_even[j], gdn_conv[j],
                                gdn_a_log[j], gdn_dt_bias[j], gdn_norm[j], diff_lambda[j],
                                diff_norm[j], 0.8 - 0.6 * math.exp(-0.3 * i), not last)
        else:
            y_x = mixer_cd(xn, w_in_odd[j], w_out_odd[j], conf_dw[j], conf_dw_b[j],
                           conf_ln_g[j], conf_ln_b[j], sc_conv[j])
            if not last:
                cn = modulate(rmsnorm(h, g_mix_pre[i]), mc[0], mc[1])
                y_c = mixer_cd(cn, w_in_odd[j], w_out_odd[j], conf_dw[j], conf_dw_b[j],
                               conf_ln_g[j], conf_ln_b[j], sc_conv[j])
        x = x + mx[2] * rmsnorm(y_x, g_mix_post[i])
        f_x = swiglu(modulate(rmsnorm(x, g_ffn_pre[i]), mx[3], mx[4]), w_ff_gate[i], w_ff_up[i], w_ff_down[i])
        x = x + mx[5] * rmsnorm(f_x, g_ffn_post[i])
        if not last:
            h = h + mc[2] * rmsnorm(y_c, g_mix_post[i])
            f_c = swiglu(modulate(rmsnorm(h, g_ffn_pre[i]), mc[3], mc[4]), w_ff_gate[i], w_ff_up[i], w_ff_down[i])
            h = h + mc[5] * rmsnorm(f_c, g_ffn_post[i])
    return x
```

```python
import functools
import math

import jax
import jax.numpy as jnp
from jax import lax
from jax.experimental import pallas as pl
from jax.experimental.pallas import tpu as pltpu

F32 = jnp.float32
BF16 = jnp.bfloat16

GRID_W = 64
GDN_HEADS = 16
GDN_D = 128
GDN_CHUNK = 64
DIFF_HEADS = 8
DIFF_D = 128
ROPE_BASE = 10000.0
CONF_K = 31
CONF_PAD = 16

LANES = 128
SUBLANES = 8
VMEM_LIMIT = 56 * 1024 * 1024
ROW_TILE = 256
MM_TILE = 1024
FF_TILE = 512
FF_ALIGN = 1024
HIGHEST = lax.Precision.HIGHEST


def _params(*sem):
    return pltpu.CompilerParams(dimension_semantics=sem, vmem_limit_bytes=VMEM_LIMIT)


def _sigmoid(x):
    return 1.0 / (1.0 + jnp.exp(-x))


def _silu(x):
    return x * _sigmoid(x)


def _dot(a, b, precision=None):
    return jnp.dot(a, b, preferred_element_type=F32, precision=precision)


def _dot_nt(a, b):
    return lax.dot_general(a, b, (((1,), (1,)), ((), ())), preferred_element_type=F32)


def _dot_tn(a, b):
    return lax.dot_general(a, b, (((0,), (0,)), ((), ())), preferred_element_type=F32)


def _adaln_kernel(c_ref, w_ref, b_ref, o_ref):
    s = _silu(c_ref[...]).astype(BF16)
    o_ref[...] = _dot(s, w_ref[...].astype(BF16)) + b_ref[...]


def adaln(cond, w, b):
    rows, d = cond.shape
    n = w.shape[1]
    tn = 512
    return pl.pallas_call(
        _adaln_kernel,
        grid=(n // tn,),
        in_specs=[pl.BlockSpec((rows, d), lambda j: (0, 0)),
                  pl.BlockSpec((d, tn), lambda j: (0, j)),
                  pl.BlockSpec((1, tn), lambda j: (0, j))],
        out_specs=pl.BlockSpec((rows, tn), lambda j: (0, j)),
        out_shape=jax.ShapeDtypeStruct((rows, n), F32),
        compiler_params=_params("parallel"),
        name="adaln",
    )(cond, w, b.reshape(1, n))


def _rms(x, eps):
    return x * lax.rsqrt(jnp.mean(x * x, axis=-1, keepdims=True) + eps)


def _prenorm_ctx_kernel(x_ref, c_ref, g_ref, sh_ref, sc_ref, o_ref):
    xv = jnp.where(pl.program_id(1) == 0, c_ref[...], x_ref[...])
    y = _rms(xv, 1e-6) * g_ref[...]
    o_ref[...] = (y * (1.0 + sc_ref[...]) + sh_ref[...]).astype(o_ref.dtype)


def prenorm_with_ctx(x, ctx, g, mods, ctx_row):
    bsz, seq, d = x.shape
    n_ctx = ctx.shape[1]
    assert n_ctx == ROW_TILE
    nt = (seq + n_ctx) // ROW_TILE
    mrow = lambda b, t: jnp.where(t == 0, ctx_row, b)
    return pl.pallas_call(
        _prenorm_ctx_kernel,
        grid=(bsz, nt),
        in_specs=[pl.BlockSpec((None, ROW_TILE, d), lambda b, t: (b, jnp.maximum(t - 1, 0), 0)),
                  pl.BlockSpec((None, ROW_TILE, d), lambda b, t: (b, 0, 0)),
                  pl.BlockSpec((1, d), lambda b, t: (0, 0)),
                  pl.BlockSpec((None, None, 1, d), lambda b, t: (mrow(b, t), 0, 0, 0)),
                  pl.BlockSpec((None, None, 1, d), lambda b, t: (mrow(b, t), 1, 0, 0))],
        out_specs=pl.BlockSpec((None, ROW_TILE, d), lambda b, t: (b, t, 0)),
        out_shape=jax.ShapeDtypeStruct((bsz, seq + n_ctx, d), BF16),
        compiler_params=_params("parallel", "parallel"),
        name="prenorm_ctx",
    )(x, ctx, g.reshape(1, d), mods, mods)


def _postnorm_kernel(x_ref, y_ref, gate_ref, gpost_ref, gpre_ref, sh_ref, sc_ref, xo_ref, no_ref):
    xn = x_ref[...] + gate_ref[...] * (_rms(y_ref[...], 1e-6) * gpost_ref[...])
    xo_ref[...] = xn
    h = _rms(xn, 1e-6) * gpre_ref[...]
    no_ref[...] = (h * (1.0 + sc_ref[...]) + sh_ref[...]).astype(no_ref.dtype)


def _postnorm_last_kernel(x_ref, y_ref, gate_ref, gpost_ref, xo_ref):
    xo_ref[...] = x_ref[...] + gate_ref[...] * (_rms(y_ref[...], 1e-6) * gpost_ref[...])


def postnorm(x, y, mods, gate_idx, g_post, nxt=None):
    bsz, seq, d = x.shape
    row = pl.BlockSpec((None, ROW_TILE, d), lambda b, t: (b, t, 0))
    vec = pl.BlockSpec((1, d), lambda b, t: (0, 0))
    mod = lambda k: pl.BlockSpec((None, None, 1, d), lambda b, t: (b, k, 0, 0))
    grid = (bsz, seq // ROW_TILE)
    if nxt is None:
        return pl.pallas_call(
            _postnorm_last_kernel, grid=grid,
            in_specs=[row, row, mod(gate_idx), vec],
            out_specs=row,
            out_shape=jax.ShapeDtypeStruct((bsz, seq, d), F32),
            compiler_params=_params("parallel", "parallel"),
            name="postnorm_last",
        )(x, y, mods, g_post.reshape(1, d))
    g_pre, mods_n, shift_idx, scale_idx = nxt
    return pl.pallas_call(
        _postnorm_kernel, grid=grid,
        in_specs=[row, row, mod(gate_idx), vec, vec, mod(shift_idx), mod(scale_idx)],
        out_specs=[row, row],
        out_shape=[jax.ShapeDtypeStruct((bsz, seq, d), F32), jax.ShapeDtypeStruct((bsz, seq, d), BF16)],
        compiler_params=_params("parallel", "parallel"),
        name="postnorm",
    )(x, y, mods, g_post.reshape(1, d), g_pre.reshape(1, d), mods_n, mods_n)


def _mm_kernel(x_ref, w_ref, o_ref):
    o_ref[...] = _dot(x_ref[...], w_ref[...]).astype(o_ref.dtype)


def matmul(x, w, out_dtype, tn=MM_TILE):
    m, k = x.shape
    n = w.shape[1]
    tm = MM_TILE
    return pl.pallas_call(
        _mm_kernel,
        grid=(m // tm, n // tn),
        in_specs=[pl.BlockSpec((tm, k), lambda i, j: (i, 0)),
                  pl.BlockSpec((k, tn), lambda i, j: (0, j))],
        out_specs=pl.BlockSpec((tm, tn), lambda i, j: (i, j)),
        out_shape=jax.ShapeDtypeStruct((m, n), out_dtype),
        compiler_params=_params("parallel", "parallel"),
        name="matmul",
    )(x, w)


def _mm2_kernel(a_ref, b_ref, w_ref, o_ref):
    ka = a_ref.shape[1]
    o_ref[...] = _dot(a_ref[...], w_ref[:ka, :]) + _dot(b_ref[...], w_ref[ka:, :])


def matmul_cat(a, b, w):
    m, ka = a.shape
    kb = b.shape[1]
    n = w.shape[1]
    tm = tn = MM_TILE
    return pl.pallas_call(
        _mm2_kernel,
        grid=(m // tm, n // tn),
        in_specs=[pl.BlockSpec((tm, ka), lambda i, j: (i, 0)),
                  pl.BlockSpec((tm, kb), lambda i, j: (i, 0)),
                  pl.BlockSpec((ka + kb, tn), lambda i, j: (0, j))],
        out_specs=pl.BlockSpec((tm, tn), lambda i, j: (i, j)),
        out_shape=jax.ShapeDtypeStruct((m, n), F32),
        compiler_params=_params("parallel", "parallel"),
        name="matmul_cat",
    )(a, b, w)


def _ff_up_kernel(x_ref, wg_ref, wu_ref, o_ref):
    x = x_ref[...]
    o_ref[...] = (_silu(_dot(x, wg_ref[...])) * _dot(x, wu_ref[...])).astype(o_ref.dtype)


def ff_up(x, wg, wu):
    m, k = x.shape
    n = wg.shape[1]
    tm, tn = MM_TILE, FF_TILE
    wspec = pl.BlockSpec((k, tn), lambda i, j: (0, j))
    return pl.pallas_call(
        _ff_up_kernel,
        grid=(m // tm, n // tn),
        in_specs=[pl.BlockSpec((tm, k), lambda i, j: (i, 0)), wspec, wspec],
        out_specs=pl.BlockSpec((tm, tn), lambda i, j: (i, j)),
        out_shape=jax.ShapeDtypeStruct((m, n), BF16),
        compiler_params=_params("parallel", "parallel"),
        name="ff_up",
    )(x, wg, wu)


def _mm_acc_kernel(x_ref, w_ref, o_ref):
    part = _dot(x_ref[...], w_ref[...])

    @pl.when(pl.program_id(2) == 0)
    def _():
        o_ref[...] = part

    @pl.when(pl.program_id(2) > 0)
    def _():
        o_ref[...] += part


def matmul_ksplit(x, w, k_steps):
    m, k = x.shape
    n = w.shape[1]
    tm = tn = MM_TILE
    tk = k // k_steps
    assert tk * k_steps == k and tk % LANES == 0
    return pl.pallas_call(
        _mm_acc_kernel,
        grid=(m // tm, n // tn, k_steps),
        in_specs=[pl.BlockSpec((tm, tk), lambda i, j, kk: (i, kk)),
                  pl.BlockSpec((tk, tn), lambda i, j, kk: (kk, j))],
        out_specs=pl.BlockSpec((tm, tn), lambda i, j, kk: (i, j)),
        out_shape=jax.ShapeDtypeStruct((m, n), F32),
        compiler_params=_params("parallel", "parallel", "arbitrary"),
        name="matmul_ksplit",
    )(x, w)


def _gdn_prep_kernel(p_ref, w_ref, o_ref, *, n_ctx, n_norm_blocks):
    x = p_ref[...]
    rows = x.shape[0]
    row = lax.broadcasted_iota(jnp.int32, x.shape, 0)
    prev = jnp.where((row == 0) | (row == n_ctx), 0.0, pltpu.roll(x, 1, 0))
    nxt = jnp.where((row == n_ctx - 1) | (row == rows - 1), 0.0, pltpu.roll(x, rows - 1, 0))
    w = w_ref[...]
    y = _silu(prev * w[0:1] + x * w[1:2] + nxt * w[2:3])
    yn = y * lax.rsqrt(jnp.sum(y * y, axis=-1, keepdims=True) + 1e-6)
    o_ref[...] = jnp.where(pl.program_id(1) < n_norm_blocks, yn, y)


def gdn_prep(p, conv_w, n_ctx):
    bsz, rows, _ = p.shape
    n_blocks = conv_w.shape[1] // LANES
    kern = functools.partial(_gdn_prep_kernel, n_ctx=n_ctx, n_norm_blocks=2 * GDN_HEADS)
    return pl.pallas_call(
        kern,
        grid=(bsz, n_blocks),
        in_specs=[pl.BlockSpec((None, rows, LANES), lambda b, j: (b, 0, j)),
                  pl.BlockSpec((conv_w.shape[0], LANES), lambda b, j: (0, j))],
        out_specs=pl.BlockSpec((None, rows, LANES), lambda b, j: (b, 0, j)),
        out_shape=jax.ShapeDtypeStruct((bsz, rows, n_blocks * LANES), F32),
        compiler_params=_params("parallel", "parallel"),
        name="gdn_prep",
    )(p, conv_w)


def _gdn_gates_kernel(ab_ref, alog_ref, dt_ref, o_ref):
    x = ab_ref[...]
    rows = x.shape[0]
    lane = lax.broadcasted_iota(jnp.int32, x.shape, 1)
    r64 = lax.broadcasted_iota(jnp.int32, x.shape, 0) & (GDN_CHUNK - 1)
    z = x + dt_ref[...]
    g = -jnp.exp(alog_ref[...]) * (jnp.maximum(z, 0.0) + jnp.log1p(jnp.exp(-jnp.abs(z))))
    pre = g
    suf = g
    s = 1
    while s < GDN_CHUNK:
        pre = pre + jnp.where(r64 >= s, pltpu.roll(pre, s, 0), 0.0)
        suf = suf + jnp.where(r64 < GDN_CHUNK - s, pltpu.roll(suf, rows - s, 0), 0.0)
        s *= 2
    val = jnp.where(lane < GDN_HEADS, pre, jnp.where(lane < 2 * GDN_HEADS, suf, _sigmoid(x)))
    for j in range(4 * GDN_HEADS):
        o_ref[j] = jnp.broadcast_to(val[:, j:j + 1], (rows, LANES))


def gdn_gates(ab, a_log, dt_bias):
    bsz, rows, _ = ab.shape
    pad = jnp.zeros((LANES - 2 * GDN_HEADS,), F32)
    alog = jnp.concatenate([a_log.reshape(-1), pad]).reshape(1, LANES)
    dt = jnp.concatenate([dt_bias.reshape(-1), pad]).reshape(1, LANES)
    return pl.pallas_call(
        _gdn_gates_kernel,
        grid=(bsz, rows // ROW_TILE),
        in_specs=[pl.BlockSpec((None, ROW_TILE, LANES), lambda b, t: (b, t, 0)),
                  pl.BlockSpec((1, LANES), lambda b, t: (0, 0)),
                  pl.BlockSpec((1, LANES), lambda b, t: (0, 0))],
        out_specs=pl.BlockSpec((None, 4 * GDN_HEADS, ROW_TILE, LANES), lambda b, t: (b, 0, t, 0)),
        out_shape=jax.ShapeDtypeStruct((bsz, 4 * GDN_HEADS, rows, LANES), F32),
        compiler_params=_params("parallel", "parallel"),
        name="gdn_gates",
    )(ab, alog, dt)


def _gdn_chunk_kernel(q_ref, k_ref, v_ref, g_ref, b_ref, qd_ref, w_ref, kt_ref, u_ref, qk_ref, egl_ref):
    fwd = pl.program_id(0) == 0
    c = GDN_CHUNK
    r_iota = lax.broadcasted_iota(jnp.int32, (c, c), 0)
    c_iota = lax.broadcasted_iota(jnp.int32, (c, c), 1)
    ri = jnp.where(fwd, r_iota, c_iota)
    ci = jnp.where(fwd, c_iota, r_iota)
    incl = ri >= ci
    strict = ri > ci
    eye = (ri == ci).astype(F32)
    for n in range(q_ref.shape[0] // c):
        rs = slice(n * c, (n + 1) * c)
        q = q_ref[rs, :] * (GDN_D ** -0.5)
        k = k_ref[rs, :]
        v = v_ref[rs, :]
        g = g_ref[rs, :]
        beta = b_ref[rs, :]
        g_cols = g.T[:c, :]
        decay = jnp.where(incl, jnp.exp(jnp.where(incl, g[:, :c] - g_cols, 0.0)), 0.0)
        kb = k * beta
        kbf = k.astype(BF16)
        lmat = jnp.where(strict, _dot_nt(kb.astype(BF16), kbf) * decay, 0.0)
        tinv = eye - lmat
        pw = _dot(lmat, lmat, HIGHEST)
        for it in range(5):
            tinv = tinv + _dot(tinv, pw, HIGHEST)
            if it < 4:
                pw = _dot(pw, pw, HIGHEST)
        eg = jnp.exp(g)
        sol = _dot(tinv, jnp.concatenate([v * beta, kb * eg], axis=1), HIGHEST)
        g_last = jnp.where(fwd, g[c - 1:c, :], g[0:1, :])
        u_ref[rs, :] = sol[:, :GDN_D]
        w_ref[rs, :] = sol[:, GDN_D:].astype(BF16)
        qd_ref[rs, :] = (q * eg).astype(BF16)
        kt_ref[rs, :] = (k * jnp.exp(g_last - g)).astype(BF16)
        qk_ref[rs, :] = jnp.where(incl, _dot_nt(q.astype(BF16), kbf) * decay, 0.0).astype(BF16)
        egl_ref[n * SUBLANES:(n + 1) * SUBLANES, :] = jnp.broadcast_to(jnp.exp(g_last), (SUBLANES, LANES))


def gdn_chunks(qkv, gates):
    bsz, rows, _ = qkv.shape
    h = GDN_HEADS
    nt = rows // ROW_TILE
    cpt = ROW_TILE // GDN_CHUNK
    col = lambda off: pl.BlockSpec((None, ROW_TILE, LANES), lambda d, b, hh, t: (b, t, off + hh))
    gate = lambda off: pl.BlockSpec((None, None, ROW_TILE, LANES), lambda d, b, hh, t: (b, off + d * h + hh, t, 0))
    out = lambda w: pl.BlockSpec((None, None, None, ROW_TILE, w), lambda d, b, hh, t: (d, b, hh, t, 0))
    shp = lambda w, dt: jax.ShapeDtypeStruct((2, bsz, h, rows, w), dt)
    return pl.pallas_call(
        _gdn_chunk_kernel,
        grid=(2, bsz, h, nt),
        in_specs=[col(0), col(h), col(2 * h), gate(0), gate(2 * h)],
        out_specs=[out(LANES), out(LANES), out(LANES), out(LANES), out(GDN_CHUNK),
                   pl.BlockSpec((None, None, None, cpt * SUBLANES, LANES), lambda d, b, hh, t: (d, b, hh, t, 0))],
        out_shape=[shp(LANES, BF16), shp(LANES, BF16), shp(LANES, BF16), shp(LANES, F32), shp(GDN_CHUNK, BF16),
                   jax.ShapeDtypeStruct((2, bsz, h, rows // GDN_CHUNK * SUBLANES, LANES), F32)],
        compiler_params=_params("parallel", "parallel", "parallel", "parallel"),
        name="gdn_chunks",
    )(qkv, qkv, qkv, gates, gates)


def _gdn_scan_kernel(qd_ref, w_ref, kt_ref, u_ref, qk_ref, egl_ref, o_ref, *, n_ctx_chunks, n_chunks):
    c = GDN_CHUNK

    def step(i, carry):
        order = (i, jnp.where(i < n_ctx_chunks, n_ctx_chunks - 1 - i, n_chunks + n_ctx_chunks - 1 - i))
        new = []
        for d in range(2):
            s = carry[d]
            cidx = order[d]
            rows = pl.ds(pl.multiple_of(cidx * c, c), c)
            sb = s.astype(BF16)
            v_new = u_ref[d, rows, :] - _dot(w_ref[d, rows, :], sb)
            vb = v_new.astype(BF16)
            o = _dot(qd_ref[d, rows, :], sb) + _dot(qk_ref[d, rows, :], vb)
            decay = egl_ref[d, pl.ds(pl.multiple_of(cidx * SUBLANES, SUBLANES), 1), :]
            new.append(s * decay + _dot_tn(kt_ref[d, rows, :], vb))

            @pl.when(i >= n_ctx_chunks)
            def _():
                orow = pl.multiple_of(jnp.maximum(cidx - n_ctx_chunks, 0) * c, c)
                o_ref[d, pl.ds(orow, c), :] = o
        return tuple(new)

    zero = jnp.zeros((GDN_D, GDN_D), F32)
    lax.fori_loop(0, n_chunks, step, (zero, zero))


def gdn_scan(qd, w, kt, u, qk, egl, n_ctx):
    _, bsz, h, rows, _ = qd.shape
    seq = rows - n_ctx
    kern = functools.partial(_gdn_scan_kernel, n_ctx_chunks=n_ctx // GDN_CHUNK, n_chunks=rows // GDN_CHUNK)
    blk = lambda r, wd: pl.BlockSpec((2, None, None, r, wd), lambda b, hh: (0, b, hh, 0, 0))
    return pl.pallas_call(
        kern,
        grid=(bsz, h),
        in_specs=[blk(rows, LANES), blk(rows, LANES), blk(rows, LANES), blk(rows, LANES), blk(rows, GDN_CHUNK),
                  blk(rows // GDN_CHUNK * SUBLANES, LANES)],
        out_specs=pl.BlockSpec((2, None, seq, LANES), lambda b, hh: (0, b, 0, hh)),
        out_shape=jax.ShapeDtypeStruct((2, bsz, seq, h * LANES), F32),
        compiler_params=_params("parallel", "parallel"),
        name="gdn_scan",
    )(qd, w, kt, u, qk, egl)


def _gdn_out_kernel(o_ref, z_ref, gn_ref, y_ref):
    gn = gn_ref[...]
    for hh in range(GDN_HEADS):
        cs = slice(hh * GDN_D, (hh + 1) * GDN_D)
        o = o_ref[0, :, cs] + o_ref[1, :, cs]
        y_ref[:, cs] = (_rms(o, 1e-6) * gn * _silu(z_ref[:, cs])).astype(y_ref.dtype)


def gdn_out(o, p, z_block, row_off_blocks, gn):
    _, bsz, seq, width = o.shape
    return pl.pallas_call(
        _gdn_out_kernel,
        grid=(bsz, seq // ROW_TILE),
        in_specs=[pl.BlockSpec((2, None, ROW_TILE, width), lambda b, t: (0, b, t, 0)),
                  pl.BlockSpec((None, ROW_TILE, width), lambda b, t: (b, t + row_off_blocks, z_block)),
                  pl.BlockSpec((1, GDN_D), lambda b, t: (0, 0))],
        out_specs=pl.BlockSpec((None, ROW_TILE, width), lambda b, t: (b, t, 0)),
        out_shape=jax.ShapeDtypeStruct((bsz, seq, width), BF16),
        compiler_params=_params("parallel", "parallel"),
        name="gdn_out",
    )(o, p, gn.reshape(1, GDN_D))


def _rope_kernel(q_ref, k_ref, v_ref, cos_ref, sa_ref, sb_ref, qo_ref, ko_ref, vo_ref):
    cos, sa, sb = cos_ref[...], sa_ref[...], sb_ref[...]

    def rope(x):
        return x * cos + pltpu.roll(x, LANES - DIFF_D // 4, 1) * sa + pltpu.roll(x, DIFF_D // 4, 1) * sb

    for j in range(q_ref.shape[1] // DIFF_D):
        cs = slice(j * DIFF_D, (j + 1) * DIFF_D)
        qo_ref[:, cs] = rope(q_ref[:, cs] * (DIFF_D ** -0.5)).astype(qo_ref.dtype)
        ko_ref[:, cs] = rope(k_ref[:, cs]).astype(ko_ref.dtype)
    vo_ref[...] = v_ref[...].astype(vo_ref.dtype)


def rope_tables(n_ctx, seq):
    half = DIFF_D // 2
    rows = seq // GRID_W
    inv = 1.0 / (ROPE_BASE ** (jnp.arange(0, half, 2, dtype=F32) / half))
    r = jnp.repeat(jnp.arange(rows), GRID_W).astype(F32)[:, None]
    col = jnp.tile(jnp.arange(GRID_W), rows).astype(F32)[:, None]
    ang = jnp.concatenate([r * inv, r * inv, col * inv, col * inv], axis=-1)
    cos, sin = jnp.cos(ang), jnp.sin(ang)
    first_quarter = (jnp.arange(DIFF_D) % half) < half // 2
    sin_a = jnp.where(first_quarter, -sin, 0.0)
    sin_b = jnp.where(first_quarter, 0.0, sin)
    ones = jnp.ones((n_ctx, DIFF_D), F32)
    zeros = jnp.zeros((n_ctx, DIFF_D), F32)
    cat = lambda a, b: jnp.concatenate([a, b], axis=0)
    return cat(ones, cos), cat(zeros, sin_a), cat(zeros, sin_b)


def rope_qkv(p, q_block, n_ctx, seq):
    bsz, rows, _ = p.shape
    width = DIFF_HEADS * 2 * DIFF_D
    cos, sa, sb = rope_tables(n_ctx, seq)
    blk = lambda off: pl.BlockSpec((None, ROW_TILE, width), lambda b, t: (b, t, q_block + off))
    tab = pl.BlockSpec((ROW_TILE, DIFF_D), lambda b, t: (t, 0))
    out = pl.BlockSpec((None, ROW_TILE, width), lambda b, t: (b, t, 0))
    shp = jax.ShapeDtypeStruct((bsz, rows, width), BF16)
    return pl.pallas_call(
        _rope_kernel,
        grid=(bsz, rows // ROW_TILE),
        in_specs=[blk(0), blk(1), blk(2), tab, tab, tab],
        out_specs=[out, out, out],
        out_shape=[shp, shp, shp],
        compiler_params=_params("parallel", "parallel"),
        name="rope_qkv",
    )(p, p, p, cos, sa, sb)


def _diff_attn_kernel(q_ref, k_ref, v_ref, lam_ref, dn_ref, o_ref, *, lambda_init):
    lv = lam_ref[...]
    lam = (jnp.exp(jnp.sum(lv[0:1] * lv[1:2], axis=-1, keepdims=True))
           - jnp.exp(jnp.sum(lv[2:3] * lv[3:4], axis=-1, keepdims=True)) + lambda_init)

    def probs(m):
        cs = slice(m * DIFF_D, (m + 1) * DIFF_D)
        s = _dot_nt(q_ref[:, cs], k_ref[:, cs])
        p = jnp.exp(s - jnp.max(s, axis=-1, keepdims=True))
        return p, 1.0 / jnp.sum(p, axis=-1, keepdims=True)

    p0, r0 = probs(0)
    p1, r1 = probs(1)
    a = p0 * r0 - p1 * (lam * r1)
    o = _dot(a.astype(BF16), v_ref[...])
    y = _rms(o, 1e-5) * dn_ref[...] * (1.0 - lambda_init)
    o_ref[...] = y.astype(o_ref.dtype)


def diff_attention(q, k, v, lam_vecs, diff_norm, lambda_init, n_ctx, tq=256):
    bsz, rows, width = q.shape
    seq = rows - n_ctx
    hw = 2 * DIFF_D
    kern = functools.partial(_diff_attn_kernel, lambda_init=lambda_init)
    kv = pl.BlockSpec((None, rows, hw), lambda b, h, i: (b, 0, h))
    return pl.pallas_call(
        kern,
        grid=(bsz, DIFF_HEADS, seq // tq),
        in_specs=[pl.BlockSpec((None, tq, hw), lambda b, h, i: (b, i + n_ctx // tq, h)), kv, kv,
                  pl.BlockSpec((4, DIFF_D), lambda b, h, i: (0, 0)),
                  pl.BlockSpec((1, hw), lambda b, h, i: (0, 0))],
        out_specs=pl.BlockSpec((None, tq, hw), lambda b, h, i: (b, i, h)),
        out_shape=jax.ShapeDtypeStruct((bsz, seq, width), BF16),
        compiler_params=_params("parallel", "parallel", "parallel"),
        name="diff_attention",
    )(q, k, v, lam_vecs, diff_norm.reshape(1, hw))


def _conf_conv_kernel(a_ref, b_ref, w_ref, bias_ref, o_ref, buf_ref):
    rows = a_ref.shape[0]
    zeros = jnp.zeros((CONF_PAD, LANES), F32)
    buf_ref[0:CONF_PAD, :] = zeros
    buf_ref[CONF_PAD + rows:2 * CONF_PAD + rows, :] = zeros
    buf_ref[CONF_PAD:CONF_PAD + rows, :] = a_ref[...] * _sigmoid(b_ref[...])
    w = w_ref[...]
    bias = bias_ref[...]
    first = CONF_PAD - (CONF_K - 1) // 2

    def tile(t, carry):
        base = pl.multiple_of(t * ROW_TILE, ROW_TILE)
        acc = jnp.broadcast_to(bias, (ROW_TILE, LANES))
        for kk in range(CONF_K):
            acc = acc + w[kk:kk + 1, :] * buf_ref[pl.ds(base + first + kk, ROW_TILE), :]
        o_ref[pl.ds(base, ROW_TILE), :] = acc
        return carry

    lax.fori_loop(0, rows // ROW_TILE, tile, 0)


def conf_conv(p, dw, dw_b):
    bsz, rows, _ = p.shape
    ch = dw.shape[1]
    nb = ch // LANES
    return pl.pallas_call(
        _conf_conv_kernel,
        grid=(bsz, nb),
        in_specs=[pl.BlockSpec((None, rows, LANES), lambda b, j: (b, 0, j)),
                  pl.BlockSpec((None, rows, LANES), lambda b, j: (b, 0, nb + j)),
                  pl.BlockSpec((CONF_K, LANES), lambda b, j: (0, j)),
                  pl.BlockSpec((1, LANES), lambda b, j: (0, j))],
        out_specs=pl.BlockSpec((None, rows, LANES), lambda b, j: (b, 0, j)),
        out_shape=jax.ShapeDtypeStruct((bsz, rows, ch), F32),
        scratch_shapes=[pltpu.VMEM((rows + 2 * CONF_PAD, LANES), F32)],
        compiler_params=_params("parallel", "parallel"),
        name="conf_conv",
    )(p, p, dw, dw_b.reshape(1, ch))


def _ln_silu_kernel(x_ref, g_ref, b_ref, o_ref):
    x = x_ref[...]
    xc = x - jnp.mean(x, axis=-1, keepdims=True)
    y = xc * lax.rsqrt(jnp.mean(xc * xc, axis=-1, keepdims=True) + 1e-5) * g_ref[...] + b_ref[...]
    o_ref[...] = _silu(y).astype(o_ref.dtype)


def ln_silu(x, g, b):
    bsz, rows, ch = x.shape
    row = pl.BlockSpec((None, ROW_TILE, ch), lambda bb, t: (bb, t, 0))
    vec = pl.BlockSpec((1, ch), lambda bb, t: (0, 0))
    return pl.pallas_call(
        _ln_silu_kernel,
        grid=(bsz, rows // ROW_TILE),
        in_specs=[row, vec, vec],
        out_specs=row,
        out_shape=jax.ShapeDtypeStruct((bsz, rows, ch), BF16),
        compiler_params=_params("parallel", "parallel"),
        name="ln_silu",
    )(x, g.reshape(1, ch), b.reshape(1, ch))


def _short_conv_kernel(gb_ref, gc_ref, sh_ref, w_ref, o_ref):
    m = gc_ref[...] * sh_ref[...]
    rows = m.shape[0]
    row = lax.broadcasted_iota(jnp.int32, m.shape, 0)
    prev = jnp.where(row == 0, 0.0, pltpu.roll(m, 1, 0))
    nxt = jnp.where(row == rows - 1, 0.0, pltpu.roll(m, rows - 1, 0))
    w = w_ref[...]
    o_ref[...] = (gb_ref[...] * (prev * w[0:1] + m * w[1:2] + nxt * w[2:3])).astype(o_ref.dtype)


def short_conv(p, first_block, w):
    bsz, rows, _ = p.shape
    ch = w.shape[1]
    nb = ch // LANES
    blk = lambda off: pl.BlockSpec((None, rows, LANES), lambda b, j: (b, 0, first_block + off + j))
    return pl.pallas_call(
        _short_conv_kernel,
        grid=(bsz, nb),
        in_specs=[blk(0), blk(nb), blk(2 * nb), pl.BlockSpec((w.shape[0], LANES), lambda b, j: (0, j))],
        out_specs=pl.BlockSpec((None, rows, LANES), lambda b, j: (b, 0, j)),
        out_shape=jax.ShapeDtypeStruct((bsz, rows, ch), BF16),
        compiler_params=_params("parallel", "parallel"),
        name="short_conv",
    )(p, p, p, w)


def _ffn(xn, wg, wu, wd):
    d, dff = wg.shape
    pad = -dff % FF_ALIGN
    wg = jnp.pad(wg.astype(BF16), ((0, 0), (0, pad)))
    wu = jnp.pad(wu.astype(BF16), ((0, 0), (0, pad)))
    wd = jnp.pad(wd.astype(BF16), ((0, pad), (0, 0)))
    act = ff_up(xn, wg, wu)
    return matmul_ksplit(act, wd, k_steps=4)


def _mixer_even(xn_all, n_ctx, seq, w_in, w_out, conv_w, a_log, dt_bias, gdn_norm, lam_vecs, diff_norm,
                lambda_init):
    bsz, rows, d = xn_all.shape
    qk_w = GDN_HEADS * GDN_D
    c_ab = 3 * qk_w + qk_w
    n_ab = 4 * GDN_HEADS
    w_main = jnp.concatenate([w_in[:, :c_ab], w_in[:, c_ab + n_ab:]], axis=1).astype(BF16)
    w_ab = jnp.pad(w_in[:, c_ab:c_ab + n_ab], ((0, 0), (0, LANES - n_ab))).astype(BF16)
    x2 = xn_all.reshape(bsz * rows, d)
    p = matmul(x2, w_main, F32).reshape(bsz, rows, -1)
    ab = matmul(x2, w_ab, F32, tn=LANES).reshape(bsz, rows, LANES)
    qkv = gdn_prep(p, conv_w, n_ctx)
    gates = gdn_gates(ab, a_log, dt_bias)
    qd, w, kt, u, qk, egl = gdn_chunks(qkv, gates)
    o = gdn_scan(qd, w, kt, u, qk, egl, n_ctx)
    wide = DIFF_HEADS * 2 * DIFF_D
    ya = gdn_out(o, p, c_ab // wide - 1, n_ctx // ROW_TILE, gdn_norm)
    qr, kr, vr = rope_qkv(p, c_ab // wide, n_ctx, seq)
    yb = diff_attention(qr, kr, vr, lam_vecs, diff_norm, lambda_init, n_ctx)
    y = matmul_cat(ya.reshape(bsz * seq, -1), yb.reshape(bsz * seq, -1), w_out.astype(BF16))
    return y.reshape(bsz, seq, d)


def _mixer_odd(xn, w_in, w_out, conf_dw, conf_dw_b, conf_ln_g, conf_ln_b, sc_w):
    bsz, seq, d = xn.shape
    p = matmul(xn.reshape(bsz * seq, d), w_in.astype(BF16), F32).reshape(bsz, seq, -1)
    yc = ln_silu(conf_conv(p, conf_dw, conf_dw_b), conf_ln_g, conf_ln_b)
    yd = short_conv(p, 2 * (conf_dw.shape[1] // LANES), sc_w)
    y = matmul_cat(yc.reshape(bsz * seq, -1), yd.reshape(bsz * seq, -1), w_out.astype(BF16))
    return y.reshape(bsz, seq, d)


@jax.jit
def _forward(x, c, ctx, c_ctx, w_ada, b_ada, g_mix_pre, g_mix_post, g_ffn_pre, g_ffn_post,
             w_ff_gate, w_ff_up, w_ff_down, w_in_even, w_out_even, gdn_conv, gdn_a_log,
             gdn_dt_bias, gdn_norm, diff_lambda, diff_norm, w_in_odd, w_out_odd, conf_dw,
             conf_dw_b, conf_ln_g, conf_ln_b, sc_conv):
    bsz, seq, d = x.shape
    n_ctx = ctx.shape[1]
    depth = w_ada.shape[0]
    assert depth == 2, "the context stream is only carried through the first (even) layer"
    cond = jnp.concatenate([c, c_ctx[None, :], jnp.zeros((SUBLANES - bsz - 1, d), F32)], axis=0)
    mods = [adaln(cond, w_ada[i], b_ada[i]).reshape(SUBLANES, 6, 1, d) for i in range(depth)]

    xn_all = prenorm_with_ctx(x, ctx, g_mix_pre[0], mods[0], bsz)
    y = _mixer_even(xn_all, n_ctx, seq, w_in_even[0], w_out_even[0], gdn_conv[0], gdn_a_log[0],
                    gdn_dt_bias[0], gdn_norm[0], diff_lambda[0], diff_norm[0], 0.8 - 0.6 * math.exp(-0.3 * 0))
    x, xn = postnorm(x, y, mods[0], 2, g_mix_post[0], (g_ffn_pre[0], mods[0], 3, 4))
    f = _ffn(xn.reshape(bsz * seq, d), w_ff_gate[0], w_ff_up[0], w_ff_down[0]).reshape(bsz, seq, d)
    x, xn = postnorm(x, f, mods[0], 5, g_ffn_post[0], (g_mix_pre[1], mods[1], 0, 1))

    y = _mixer_odd(xn, w_in_odd[0], w_out_odd[0], conf_dw[0], conf_dw_b[0], conf_ln_g[0], conf_ln_b[0],
                   sc_conv[0])
    x, xn = postnorm(x, y, mods[1], 2, g_mix_post[1], (g_ffn_pre[1], mods[1], 3, 4))
    f = _ffn(xn.reshape(bsz * seq, d), w_ff_gate[1], w_ff_up[1], w_ff_down[1]).reshape(bsz, seq, d)
    return postnorm(x, f, mods[1], 5, g_ffn_post[1])


def kernel(x, c, ctx, c_ctx, w_ada, b_ada, g_mix_pre, g_mix_post, g_ffn_pre, g_ffn_post, w_ff_gate, w_ff_up,
           w_ff_down, w_in_even, w_out_even, gdn_conv, gdn_a_log, gdn_dt_bias, gdn_norm, diff_lambda, diff_norm,
           w_in_odd, w_out_odd, conf_dw, conf_dw_b, conf_ln_g, conf_ln_b, sc_conv):
    return _forward(x, c, ctx, c_ctx, w_ada, b_ada, g_mix_pre, g_mix_post, g_ffn_pre, g_ffn_post, w_ff_gate,
                    w_ff_up, w_ff_down, w_in_even, w_out_even, gdn_conv, gdn_a_log, gdn_dt_bias, gdn_norm,
                    diff_lambda, diff_norm, w_in_odd, w_out_odd, conf_dw, conf_dw_b, conf_ln_g, conf_ln_b,
                    sc_conv)
```

```python
import functools
import math

import jax
import jax.numpy as jnp
from jax import lax
from jax.experimental import pallas as pl
from jax.experimental.pallas import tpu as pltpu

F32 = jnp.float32
BF16 = jnp.bfloat16

GRID_W = 64
GDN_HEADS = 16
GDN_D = 128
GDN_CHUNK = 64
DIFF_HEADS = 8
DIFF_D = 128
ROPE_BASE = 10000.0
CONF_K = 31
CONF_PAD = 16

LANES = 128
SUBLANES = 8
VMEM_LIMIT = 56 * 1024 * 1024
ROW_TILE = 256
MM_TILE = 1024
FF_TILE = 512
FF_ALIGN = 1024


def _params(*sem):
    return pltpu.CompilerParams(dimension_semantics=sem, vmem_limit_bytes=VMEM_LIMIT)


def _sigmoid(x):
    return 1.0 / (1.0 + jnp.exp(-x))


def _silu(x):
    return x * _sigmoid(x)


def _dot(a, b):
    return jnp.dot(a, b, preferred_element_type=F32)


def _dot_nt(a, b):
    return lax.dot_general(a, b, (((1,), (1,)), ((), ())), preferred_element_type=F32)


def _dot_tn(a, b):
    return lax.dot_general(a, b, (((0,), (0,)), ((), ())), preferred_element_type=F32)


def _adaln_kernel(c_ref, w_ref, b_ref, o_ref):
    s = _silu(c_ref[...]).astype(BF16)
    o_ref[...] = _dot(s, w_ref[...].astype(BF16)) + b_ref[...]


def adaln(cond, w, b, layer):
    rows, d = cond.shape
    depth, _, n = w.shape
    tn = 512
    return pl.pallas_call(
        _adaln_kernel,
        grid=(n // tn,),
        in_specs=[pl.BlockSpec((rows, d), lambda j: (0, 0)),
                  pl.BlockSpec((None, d, tn), lambda j: (layer, 0, j)),
                  pl.BlockSpec((None, 1, tn), lambda j: (layer, 0, j))],
        out_specs=pl.BlockSpec((rows, tn), lambda j: (0, j)),
        out_shape=jax.ShapeDtypeStruct((rows, n), F32),
        compiler_params=_params("parallel"),
        name="adaln",
    )(cond, w, b.reshape(depth, 1, n))


def _rms(x, eps):
    return x * lax.rsqrt(jnp.mean(x * x, axis=-1, keepdims=True) + eps)


def _prenorm_ctx_kernel(x_ref, c_ref, g_ref, sh_ref, sc_ref, o_ref):
    xv = jnp.where(pl.program_id(1) == 0, c_ref[...], x_ref[...])
    y = _rms(xv, 1e-6) * g_ref[...]
    o_ref[...] = (y * (1.0 + sc_ref[...]) + sh_ref[...]).astype(o_ref.dtype)


def prenorm_with_ctx(x, ctx, g, mods, ctx_row):
    bsz, seq, d = x.shape
    n_ctx = ctx.shape[1]
    assert n_ctx == ROW_TILE
    nt = (seq + n_ctx) // ROW_TILE
    mrow = lambda b, t: jnp.where(t == 0, ctx_row, b)
    return pl.pallas_call(
        _prenorm_ctx_kernel,
        grid=(bsz, nt),
        in_specs=[pl.BlockSpec((None, ROW_TILE, d), lambda b, t: (b, jnp.maximum(t - 1, 0), 0)),
                  pl.BlockSpec((None, ROW_TILE, d), lambda b, t: (b, 0, 0)),
                  pl.BlockSpec((1, d), lambda b, t: (0, 0)),
                  pl.BlockSpec((None, None, 1, d), lambda b, t: (mrow(b, t), 0, 0, 0)),
                  pl.BlockSpec((None, None, 1, d), lambda b, t: (mrow(b, t), 1, 0, 0))],
        out_specs=pl.BlockSpec((None, ROW_TILE, d), lambda b, t: (b, t, 0)),
        out_shape=jax.ShapeDtypeStruct((bsz, seq + n_ctx, d), BF16),
        compiler_params=_params("parallel", "parallel"),
        name="prenorm_ctx",
    )(x, ctx, g.reshape(1, d), mods, mods)


def _postnorm_kernel(x_ref, y_ref, gate_ref, gpost_ref, gpre_ref, sh_ref, sc_ref, xo_ref, no_ref):
    xn = x_ref[...] + gate_ref[...] * (_rms(y_ref[...], 1e-6) * gpost_ref[...])
    xo_ref[...] = xn
    h = _rms(xn, 1e-6) * gpre_ref[...]
    no_ref[...] = (h * (1.0 + sc_ref[...]) + sh_ref[...]).astype(no_ref.dtype)


def _postnorm_last_kernel(x_ref, y_ref, gate_ref, gpost_ref, xo_ref):
    xo_ref[...] = x_ref[...] + gate_ref[...] * (_rms(y_ref[...], 1e-6) * gpost_ref[...])


def postnorm(x, y, mods, gate_idx, g_post, nxt=None):
    bsz, seq, d = x.shape
    row = pl.BlockSpec((None, ROW_TILE, d), lambda b, t: (b, t, 0))
    vec = pl.BlockSpec((1, d), lambda b, t: (0, 0))
    mod = lambda k: pl.BlockSpec((None, None, 1, d), lambda b, t: (b, k, 0, 0))
    grid = (bsz, seq // ROW_TILE)
    if nxt is None:
        return pl.pallas_call(
            _postnorm_last_kernel, grid=grid,
            in_specs=[row, row, mod(gate_idx), vec],
            out_specs=row,
            out_shape=jax.ShapeDtypeStruct((bsz, seq, d), F32),
            compiler_params=_params("parallel", "parallel"),
            name="postnorm_last",
        )(x, y, mods, g_post.reshape(1, d))
    g_pre, mods_n, shift_idx, scale_idx = nxt
    return pl.pallas_call(
        _postnorm_kernel, grid=grid,
        in_specs=[row, row, mod(gate_idx), vec, vec, mod(shift_idx), mod(scale_idx)],
        out_specs=[row, row],
        out_shape=[jax.ShapeDtypeStruct((bsz, seq, d), F32), jax.ShapeDtypeStruct((bsz, seq, d), BF16)],
        compiler_params=_params("parallel", "parallel"),
        name="postnorm",
    )(x, y, mods, g_post.reshape(1, d), g_pre.reshape(1, d), mods_n, mods_n)


def _cast_pad_kernel(w_ref, o_ref, *, n_row_tiles):
    cols = w_ref.shape[1]

    @pl.when(pl.program_id(0) < n_row_tiles)
    def _():
        o_ref[:, :cols] = w_ref[...].astype(o_ref.dtype)
        if o_ref.shape[1] > cols:
            o_ref[:, cols:] = jnp.zeros((o_ref.shape[0], o_ref.shape[1] - cols), o_ref.dtype)

    @pl.when(pl.program_id(0) >= n_row_tiles)
    def _():
        o_ref[...] = jnp.zeros(o_ref.shape, o_ref.dtype)


def cast_weight(w, layer, n_cols=None, pad_rows=0, pad_cols=0):
    _, rows, cols = w.shape
    cols = cols if n_cols is None else n_cols
    assert rows % ROW_TILE == 0 and pad_rows % ROW_TILE == 0 and cols % LANES == 0 and pad_cols % LANES == 0
    n_row_tiles = rows // ROW_TILE
    kern = functools.partial(_cast_pad_kernel, n_row_tiles=n_row_tiles)
    return pl.pallas_call(
        kern,
        grid=((rows + pad_rows) // ROW_TILE,),
        in_specs=[pl.BlockSpec((None, ROW_TILE, cols), lambda i: (layer, jnp.minimum(i, n_row_tiles - 1), 0))],
        out_specs=pl.BlockSpec((ROW_TILE, cols + pad_cols), lambda i: (i, 0)),
        out_shape=jax.ShapeDtypeStruct((rows + pad_rows, cols + pad_cols), BF16),
        compiler_params=_params("parallel"),
        name="cast_weight",
    )(w)


def _mm_kernel(x_ref, w_ref, o_ref):
    o_ref[...] = _dot(x_ref[...], w_ref[...]).astype(o_ref.dtype)


def matmul(x, w, out_dtype, tn=MM_TILE):
    m, k = x.shape
    n = w.shape[1]
    tm = MM_TILE
    return pl.pallas_call(
        _mm_kernel,
        grid=(m // tm, n // tn),
        in_specs=[pl.BlockSpec((tm, k), lambda i, j: (i, 0)),
                  pl.BlockSpec((k, tn), lambda i, j: (0, j))],
        out_specs=pl.BlockSpec((tm, tn), lambda i, j: (i, j)),
        out_shape=jax.ShapeDtypeStruct((m, n), out_dtype),
        compiler_params=_params("parallel", "parallel"),
        name="matmul",
    )(x, w)


def _mm2_kernel(a_ref, b_ref, w_ref, o_ref):
    ka = a_ref.shape[1]
    o_ref[...] = _dot(a_ref[...], w_ref[:ka, :]) + _dot(b_ref[...], w_ref[ka:, :])


def matmul_cat(a, b, w):
    m, ka = a.shape
    kb = b.shape[1]
    n = w.shape[1]
    tm = tn = MM_TILE
    return pl.pallas_call(
        _mm2_kernel,
        grid=(m // tm, n // tn),
        in_specs=[pl.BlockSpec((tm, ka), lambda i, j: (i, 0)),
                  pl.BlockSpec((tm, kb), lambda i, j: (i, 0)),
                  pl.BlockSpec((ka + kb, tn), lambda i, j: (0, j))],
        out_specs=pl.BlockSpec((tm, tn), lambda i, j: (i, j)),
        out_shape=jax.ShapeDtypeStruct((m, n), F32),
        compiler_params=_params("parallel", "parallel"),
        name="matmul_cat",
    )(a, b, w)


def _ff_up_kernel(x_ref, wg_ref, wu_ref, o_ref):
    x = x_ref[...]
    o_ref[...] = (_silu(_dot(x, wg_ref[...])) * _dot(x, wu_ref[...])).astype(o_ref.dtype)


def ff_up(x, wg, wu):
    m, k = x.shape
    n = wg.shape[1]
    tm, tn = MM_TILE, FF_TILE
    wspec = pl.BlockSpec((k, tn), lambda i, j: (0, j))
    return pl.pallas_call(
        _ff_up_kernel,
        grid=(m // tm, n // tn),
        in_specs=[pl.BlockSpec((tm, k), lambda i, j: (i, 0)), wspec, wspec],
        out_specs=pl.BlockSpec((tm, tn), lambda i, j: (i, j)),
        out_shape=jax.ShapeDtypeStruct((m, n), BF16),
        compiler_params=_params("parallel", "parallel"),
        name="ff_up",
    )(x, wg, wu)


def _mm_acc_kernel(x_ref, w_ref, o_ref):
    part = _dot(x_ref[...], w_ref[...])

    @pl.when(pl.program_id(2) == 0)
    def _():
        o_ref[...] = part

    @pl.when(pl.program_id(2) > 0)
    def _():
        o_ref[...] += part


def matmul_ksplit(x, w, k_steps):
    m, k = x.shape
    n = w.shape[1]
    tm = tn = MM_TILE
    tk = k // k_steps
    assert tk * k_steps == k and tk % LANES == 0
    return pl.pallas_call(
        _mm_acc_kernel,
        grid=(m // tm, n // tn, k_steps),
        in_specs=[pl.BlockSpec((tm, tk), lambda i, j, kk: (i, kk)),
                  pl.BlockSpec((tk, tn), lambda i, j, kk: (kk, j))],
        out_specs=pl.BlockSpec((tm, tn), lambda i, j, kk: (i, j)),
        out_shape=jax.ShapeDtypeStruct((m, n), F32),
        compiler_params=_params("parallel", "parallel", "arbitrary"),
        name="matmul_ksplit",
    )(x, w)


def _gdn_prep_kernel(p_ref, w_ref, o_ref, *, n_ctx, n_norm_blocks):
    x = p_ref[...].astype(F32)
    rows = x.shape[0]
    row = lax.broadcasted_iota(jnp.int32, x.shape, 0)
    prev = jnp.where((row == 0) | (row == n_ctx), 0.0, pltpu.roll(x, 1, 0))
    nxt = jnp.where((row == n_ctx - 1) | (row == rows - 1), 0.0, pltpu.roll(x, rows - 1, 0))
    w = w_ref[...]
    y = _silu(prev * w[0:1] + x * w[1:2] + nxt * w[2:3])
    yn = y * lax.rsqrt(jnp.sum(y * y, axis=-1, keepdims=True) + 1e-6)
    o_ref[...] = jnp.where(pl.program_id(1) < n_norm_blocks, yn, y)


def gdn_prep(p, conv_w, n_ctx):
    bsz, rows, _ = p.shape
    n_blocks = conv_w.shape[1] // LANES
    kern = functools.partial(_gdn_prep_kernel, n_ctx=n_ctx, n_norm_blocks=2 * GDN_HEADS)
    return pl.pallas_call(
        kern,
        grid=(bsz, n_blocks),
        in_specs=[pl.BlockSpec((None, rows, LANES), lambda b, j: (b, 0, j)),
                  pl.BlockSpec((conv_w.shape[0], LANES), lambda b, j: (0, j))],
        out_specs=pl.BlockSpec((None, rows, LANES), lambda b, j: (b, 0, j)),
        out_shape=jax.ShapeDtypeStruct((bsz, rows, n_blocks * LANES), F32),
        compiler_params=_params("parallel", "parallel"),
        name="gdn_prep",
    )(p, conv_w)


def _gdn_gates_kernel(ab_ref, alog_ref, dt_ref, o_ref):
    x = ab_ref[...]
    rows = x.shape[0]
    lane = lax.broadcasted_iota(jnp.int32, x.shape, 1)
    r64 = lax.broadcasted_iota(jnp.int32, x.shape, 0) & (GDN_CHUNK - 1)
    z = x + dt_ref[...]
    g = -jnp.exp(alog_ref[...]) * (jnp.maximum(z, 0.0) + jnp.log1p(jnp.exp(-jnp.abs(z))))
    pre = g
    suf = g
    s = 1
    while s < GDN_CHUNK:
        pre = pre + jnp.where(r64 >= s, pltpu.roll(pre, s, 0), 0.0)
        suf = suf + jnp.where(r64 < GDN_CHUNK - s, pltpu.roll(suf, rows - s, 0), 0.0)
        s *= 2
    val = jnp.where(lane < GDN_HEADS, pre, jnp.where(lane < 2 * GDN_HEADS, suf, _sigmoid(x)))
    for j in range(4 * GDN_HEADS):
        o_ref[j] = jnp.broadcast_to(val[:, j:j + 1], (rows, LANES))


def gdn_gates(ab, a_log, dt_bias):
    bsz, rows, _ = ab.shape
    pad = jnp.zeros((LANES - 2 * GDN_HEADS,), F32)
    alog = jnp.concatenate([a_log.reshape(-1), pad]).reshape(1, LANES)
    dt = jnp.concatenate([dt_bias.reshape(-1), pad]).reshape(1, LANES)
    return pl.pallas_call(
        _gdn_gates_kernel,
        grid=(bsz, rows // ROW_TILE),
        in_specs=[pl.BlockSpec((None, ROW_TILE, LANES), lambda b, t: (b, t, 0)),
                  pl.BlockSpec((1, LANES), lambda b, t: (0, 0)),
                  pl.BlockSpec((1, LANES), lambda b, t: (0, 0))],
        out_specs=pl.BlockSpec((None, 4 * GDN_HEADS, ROW_TILE, LANES), lambda b, t: (b, 0, t, 0)),
        out_shape=jax.ShapeDtypeStruct((bsz, 4 * GDN_HEADS, rows, LANES), F32),
        compiler_params=_params("parallel", "parallel"),
        name="gdn_gates",
    )(ab, alog, dt)


def _gdn_chains(chains):
    c = GDN_CHUNK
    n_chunks = chains[0][0].shape[0] // c
    r_iota = lax.broadcasted_iota(jnp.int32, (c, c), 0)
    c_iota = lax.broadcasted_iota(jnp.int32, (c, c), 1)
    eye = (r_iota == c_iota).astype(F32)
    masks = {True: (r_iota >= c_iota, r_iota > c_iota), False: (r_iota <= c_iota, r_iota < c_iota)}

    items = []
    for (q, k, v, g, beta, _, fwd) in chains:
        incl, strict = masks[fwd]
        for n in range(n_chunks):
            rs = slice(n * c, (n + 1) * c)
            qn = q[rs, :] * (GDN_D ** -0.5)
            kn, gn, bn = k[rs, :], g[rs, :], beta[rs, :]
            g_cols = gn.T[:c, :]
            decay = jnp.where(incl, jnp.exp(jnp.where(incl, gn[:, :c] - g_cols, 0.0)), 0.0)
            kb = kn * bn
            eg = jnp.exp(gn)
            g_last = gn[c - 1:c, :] if fwd else gn[0:1, :]
            items.append(dict(
                incl=incl, strict=strict, decay=decay, kbf=kn.astype(BF16),
                lhs=jnp.concatenate([kb, qn], axis=0).astype(BF16),
                rhs=jnp.concatenate([v[rs, :] * bn, kb * eg], axis=1).astype(BF16),
                qd=qn * eg, kt=(kn * jnp.exp(g_last - gn)).astype(BF16), e_last=jnp.exp(g_last)))
    for it in items:
        it["both"] = _dot_nt(it["lhs"], it["kbf"])
    for it in items:
        lmat = jnp.where(it["strict"], it["both"][:c, :] * it["decay"], 0.0)
        it["qk"] = jnp.where(it["incl"], it["both"][c:, :] * it["decay"], 0.0).astype(BF16)
        it["tinv"] = eye - lmat
        it["pb"] = lmat.astype(BF16)
    for it in items:
        it["pb"] = _dot(it["pb"], it["pb"]).astype(BF16)
    for stage in range(5):
        for it in items:
            it["tinv"] = it["tinv"] + _dot(it["tinv"].astype(BF16), it["pb"])
            if stage < 4:
                it["pb"] = _dot(it["pb"], it["pb"]).astype(BF16)
    for it in items:
        sol = _dot(it["tinv"].astype(BF16), it["rhs"])
        it["u"] = sol[:, :GDN_D]
        it["wq"] = jnp.concatenate([sol[:, GDN_D:], it["qd"]], axis=0).astype(BF16)

    states = [ch[5] for ch in chains]
    outs = [[None] * n_chunks for _ in chains]
    for step in range(n_chunks):
        cur = [items[ci * n_chunks + (step if ch[6] else n_chunks - 1 - step)] for ci, ch in enumerate(chains)]
        ws = [_dot(it["wq"], s.astype(BF16)) for it, s in zip(cur, states)]
        vb = [(it["u"] - w[:c, :]).astype(BF16) for it, w in zip(cur, ws)]
        for ci, (ch, it) in enumerate(zip(chains, cur)):
            n = step if ch[6] else n_chunks - 1 - step
            outs[ci][n] = ws[ci][c:, :] + _dot(it["qk"], vb[ci])
            states[ci] = states[ci] * it["e_last"] + _dot_tn(it["kt"], vb[ci])
    return [(jnp.concatenate(o, axis=0), s) for o, s in zip(outs, states)]


def _gdn_kernel(qf_ref, kf_ref, vf_ref, qb_ref, kb_ref, vb_ref, gf_ref, bf_ref, gb_ref, bb_ref,
                of_ref, ob_ref, s_ref):
    t = pl.program_id(2)

    @pl.when(t == 0)
    def _():
        s_ref[...] = jnp.zeros(s_ref.shape, F32)

    dirs = ((qf_ref, kf_ref, vf_ref, gf_ref, bf_ref, of_ref), (qb_ref, kb_ref, vb_ref, gb_ref, bb_ref, ob_ref))
    chains, dests = [], []
    for hh in range(gf_ref.shape[0]):
        cs = slice(hh * GDN_D, (hh + 1) * GDN_D)
        for d, (q_ref, k_ref, v_ref, g_ref, b_ref, o_ref) in enumerate(dirs):
            chains.append((q_ref[:, cs].astype(F32), k_ref[:, cs].astype(F32), v_ref[:, cs].astype(F32),
                           g_ref[hh], b_ref[hh], s_ref[2 * hh + d], d == 0))
            dests.append((o_ref, cs, 2 * hh + d))
    for (o_ref, cs, idx), (o, s_new) in zip(dests, _gdn_chains(chains)):
        s_ref[idx] = s_new
        o_ref[:, cs] = o


def gdn_scan(qkv, gates, n_ctx, heads_per_step=4):
    bsz, rows, _ = qkv.shape
    assert n_ctx == ROW_TILE
    h, hb = GDN_HEADS, heads_per_step
    nt = rows // ROW_TILE
    seq = rows - n_ctx
    wid = hb * GDN_D
    tile_f = lambda t: t
    tile_b = lambda t: jnp.where(t == 0, 0, nt - t)
    out_f = lambda t: jnp.maximum(t - 1, 0)
    out_b = lambda t: jnp.where(t == 0, nt - 2, nt - 1 - t)
    col = lambda part, tile: pl.BlockSpec((None, ROW_TILE, wid), lambda b, hg, t: (b, tile(t), part * (h // hb) + hg))
    gate = lambda kind, tile: pl.BlockSpec((None, hb, ROW_TILE, LANES),
                                           lambda b, hg, t: (b, kind * (h // hb) + hg, tile(t), 0))
    out = lambda tile: pl.BlockSpec((None, ROW_TILE, wid), lambda b, hg, t: (b, tile(t), hg))
    shp = jax.ShapeDtypeStruct((bsz, seq, h * GDN_D), F32)
    return pl.pallas_call(
        _gdn_kernel,
        grid=(bsz, h // hb, nt),
        in_specs=[col(0, tile_f), col(1, tile_f), col(2, tile_f), col(0, tile_b), col(1, tile_b), col(2, tile_b),
                  gate(0, tile_f), gate(2, tile_f), gate(1, tile_b), gate(3, tile_b)],
        out_specs=[out(out_f), out(out_b)],
        out_shape=[shp, shp],
        scratch_shapes=[pltpu.VMEM((2 * hb, GDN_D, GDN_D), F32)],
        compiler_params=_params("parallel", "parallel", "arbitrary"),
        name="gdn_scan",
    )(qkv, qkv, qkv, qkv, qkv, qkv, gates, gates, gates, gates)


def _gdn_out_kernel(of_ref, ob_ref, z_ref, gn_ref, y_ref):
    gn = gn_ref[...]
    for hh in range(GDN_HEADS):
        cs = slice(hh * GDN_D, (hh + 1) * GDN_D)
        o = of_ref[:, cs] + ob_ref[:, cs]
        y_ref[:, cs] = (_rms(o, 1e-6) * gn * _silu(z_ref[:, cs].astype(F32))).astype(y_ref.dtype)


def gdn_out(o_f, o_b, p, z_block, row_off_blocks, gn):
    bsz, seq, width = o_f.shape
    row = pl.BlockSpec((None, ROW_TILE, width), lambda b, t: (b, t, 0))
    return pl.pallas_call(
        _gdn_out_kernel,
        grid=(bsz, seq // ROW_TILE),
        in_specs=[row, row,
                  pl.BlockSpec((None, ROW_TILE, width), lambda b, t: (b, t + row_off_blocks, z_block)),
                  pl.BlockSpec((1, GDN_D), lambda b, t: (0, 0))],
        out_specs=row,
        out_shape=jax.ShapeDtypeStruct((bsz, seq, width), BF16),
        compiler_params=_params("parallel", "parallel"),
        name="gdn_out",
    )(o_f, o_b, p, gn.reshape(1, GDN_D))


def _rope_kernel(q_ref, k_ref, cos_ref, sa_ref, sb_ref, qo_ref, ko_ref):
    cos, sa, sb = cos_ref[...], sa_ref[...], sb_ref[...]

    def rope(x):
        return x * cos + pltpu.roll(x, LANES - DIFF_D // 4, 1) * sa + pltpu.roll(x, DIFF_D // 4, 1) * sb

    q_scale = (DIFF_D ** -0.5) * math.log2(math.e)
    for j in range(q_ref.shape[1] // DIFF_D):
        cs = slice(j * DIFF_D, (j + 1) * DIFF_D)
        qo_ref[:, cs] = rope(q_ref[:, cs].astype(F32) * q_scale).astype(qo_ref.dtype)
        ko_ref[:, cs] = rope(k_ref[:, cs].astype(F32)).astype(ko_ref.dtype)


def rope_tables(n_ctx, seq):
    half = DIFF_D // 2
    rows = seq // GRID_W
    inv = 1.0 / (ROPE_BASE ** (jnp.arange(0, half, 2, dtype=F32) / half))
    r = jnp.repeat(jnp.arange(rows), GRID_W).astype(F32)[:, None]
    col = jnp.tile(jnp.arange(GRID_W), rows).astype(F32)[:, None]
    ang = jnp.concatenate([r * inv, r * inv, col * inv, col * inv], axis=-1)
    cos, sin = jnp.cos(ang), jnp.sin(ang)
    first_quarter = (jnp.arange(DIFF_D) % half) < half // 2
    sin_a = jnp.where(first_quarter, -sin, 0.0)
    sin_b = jnp.where(first_quarter, 0.0, sin)
    ones = jnp.ones((n_ctx, DIFF_D), F32)
    zeros = jnp.zeros((n_ctx, DIFF_D), F32)
    cat = lambda a, b: jnp.concatenate([a, b], axis=0)
    return cat(ones, cos), cat(zeros, sin_a), cat(zeros, sin_b)


def rope_qk(p, n_ctx, seq):
    bsz, rows, _ = p.shape
    width = DIFF_HEADS * 2 * DIFF_D
    cos, sa, sb = rope_tables(n_ctx, seq)
    blk = lambda off: pl.BlockSpec((None, ROW_TILE, width), lambda b, t: (b, t, off))
    tab = pl.BlockSpec((ROW_TILE, DIFF_D), lambda b, t: (t, 0))
    out = pl.BlockSpec((None, ROW_TILE, width), lambda b, t: (b, t, 0))
    shp = jax.ShapeDtypeStruct((bsz, rows, width), BF16)
    return pl.pallas_call(
        _rope_kernel,
        grid=(bsz, rows // ROW_TILE),
        in_specs=[blk(0), blk(1), tab, tab, tab],
        out_specs=[out, out],
        out_shape=[shp, shp],
        compiler_params=_params("parallel", "parallel"),
        name="rope_qk",
    )(p, p, cos, sa, sb)


def _diff_attn_kernel(q_ref, k_ref, v_ref, lam_ref, dn_ref, o_ref, *, lambda_init):
    lv = lam_ref[...]
    lam = (jnp.exp(jnp.sum(lv[0:1] * lv[1:2], axis=-1, keepdims=True))
           - jnp.exp(jnp.sum(lv[2:3] * lv[3:4], axis=-1, keepdims=True)) + lambda_init)

    def probs(m):
        cs = slice(m * DIFF_D, (m + 1) * DIFF_D)
        s = _dot_nt(q_ref[:, cs], k_ref[:, cs])
        p = jnp.exp2(s - jnp.max(s, axis=-1, keepdims=True))
        return p, 1.0 / jnp.sum(p, axis=-1, keepdims=True)

    p0, r0 = probs(0)
    p1, r1 = probs(1)
    a = p0 * r0 - p1 * (lam * r1)
    o = _dot(a.astype(BF16), v_ref[...])
    y = _rms(o, 1e-5) * dn_ref[...] * (1.0 - lambda_init)
    o_ref[...] = y.astype(o_ref.dtype)


def diff_attention(q, k, v, v_block, lam_vecs, diff_norm, lambda_init, n_ctx, tq=256):
    bsz, rows, width = q.shape
    seq = rows - n_ctx
    hw = 2 * DIFF_D
    kern = functools.partial(_diff_attn_kernel, lambda_init=lambda_init)
    kv = pl.BlockSpec((None, rows, hw), lambda b, h, i: (b, 0, h))
    return pl.pallas_call(
        kern,
        grid=(bsz, DIFF_HEADS, seq // tq),
        in_specs=[pl.BlockSpec((None, tq, hw), lambda b, h, i: (b, i + n_ctx // tq, h)), kv,
                  pl.BlockSpec((None, rows, hw), lambda b, h, i: (b, 0, v_block + h)),
                  pl.BlockSpec((4, DIFF_D), lambda b, h, i: (0, 0)),
                  pl.BlockSpec((1, hw), lambda b, h, i: (0, 0))],
        out_specs=pl.BlockSpec((None, tq, hw), lambda b, h, i: (b, i, h)),
        out_shape=jax.ShapeDtypeStruct((bsz, seq, width), BF16),
        compiler_params=_params("parallel", "parallel", "parallel"),
        name="diff_attention",
    )(q, k, v, lam_vecs, diff_norm.reshape(1, hw))


def _conf_conv_kernel(a_ref, b_ref, w_ref, bias_ref, o_ref, buf_ref):
    rows = a_ref.shape[0]
    zeros = jnp.zeros((CONF_PAD, LANES), F32)
    buf_ref[0:CONF_PAD, :] = zeros
    buf_ref[CONF_PAD + rows:2 * CONF_PAD + rows, :] = zeros
    buf_ref[CONF_PAD:CONF_PAD + rows, :] = a_ref[...].astype(F32) * _sigmoid(b_ref[...].astype(F32))
    w = w_ref[...]
    bias = bias_ref[...]
    first = CONF_PAD - (CONF_K - 1) // 2

    def tile(t, carry):
        base = pl.multiple_of(t * ROW_TILE, ROW_TILE)
        acc = jnp.broadcast_to(bias, (ROW_TILE, LANES))
        for kk in range(CONF_K):
            acc = acc + w[kk:kk + 1, :] * buf_ref[pl.ds(base + first + kk, ROW_TILE), :]
        o_ref[pl.ds(base, ROW_TILE), :] = acc
        return carry

    lax.fori_loop(0, rows // ROW_TILE, tile, 0)


def conf_conv(p, dw, dw_b):
    bsz, rows, _ = p.shape
    ch = dw.shape[1]
    nb = ch // LANES
    return pl.pallas_call(
        _conf_conv_kernel,
        grid=(bsz, nb),
        in_specs=[pl.BlockSpec((None, rows, LANES), lambda b, j: (b, 0, j)),
                  pl.BlockSpec((None, rows, LANES), lambda b, j: (b, 0, nb + j)),
                  pl.BlockSpec((CONF_K, LANES), lambda b, j: (0, j)),
                  pl.BlockSpec((1, LANES), lambda b, j: (0, j))],
        out_specs=pl.BlockSpec((None, rows, LANES), lambda b, j: (b, 0, j)),
        out_shape=jax.ShapeDtypeStruct((bsz, rows, ch), F32),
        scratch_shapes=[pltpu.VMEM((rows + 2 * CONF_PAD, LANES), F32)],
        compiler_params=_params("parallel", "parallel"),
        name="conf_conv",
    )(p, p, dw, dw_b.reshape(1, ch))


def _ln_silu_kernel(x_ref, g_ref, b_ref, o_ref):
    x = x_ref[...]
    xc = x - jnp.mean(x, axis=-1, keepdims=True)
    y = xc * lax.rsqrt(jnp.mean(xc * xc, axis=-1, keepdims=True) + 1e-5) * g_ref[...] + b_ref[...]
    o_ref[...] = _silu(y).astype(o_ref.dtype)


def ln_silu(x, g, b):
    bsz, rows, ch = x.shape
    row = pl.BlockSpec((None, ROW_TILE, ch), lambda bb, t: (bb, t, 0))
    vec = pl.BlockSpec((1, ch), lambda bb, t: (0, 0))
    return pl.pallas_call(
        _ln_silu_kernel,
        grid=(bsz, rows // ROW_TILE),
        in_specs=[row, vec, vec],
        out_specs=row,
        out_shape=jax.ShapeDtypeStruct((bsz, rows, ch), BF16),
        compiler_params=_params("parallel", "parallel"),
        name="ln_silu",
    )(x, g.reshape(1, ch), b.reshape(1, ch))


def _short_conv_kernel(gb_ref, gc_ref, sh_ref, w_ref, o_ref):
    m = gc_ref[...].astype(F32) * sh_ref[...].astype(F32)
    rows = m.shape[0]
    row = lax.broadcasted_iota(jnp.int32, m.shape, 0)
    prev = jnp.where(row == 0, 0.0, pltpu.roll(m, 1, 0))
    nxt = jnp.where(row == rows - 1, 0.0, pltpu.roll(m, rows - 1, 0))
    w = w_ref[...]
    o_ref[...] = (gb_ref[...].astype(F32) * (prev * w[0:1] + m * w[1:2] + nxt * w[2:3])).astype(o_ref.dtype)


def short_conv(p, first_block, w):
    bsz, rows, _ = p.shape
    ch = w.shape[1]
    nb = ch // LANES
    blk = lambda off: pl.BlockSpec((None, rows, LANES), lambda b, j: (b, 0, first_block + off + j))
    return pl.pallas_call(
        _short_conv_kernel,
        grid=(bsz, nb),
        in_specs=[blk(0), blk(nb), blk(2 * nb), pl.BlockSpec((w.shape[0], LANES), lambda b, j: (0, j))],
        out_specs=pl.BlockSpec((None, rows, LANES), lambda b, j: (b, 0, j)),
        out_shape=jax.ShapeDtypeStruct((bsz, rows, ch), BF16),
        compiler_params=_params("parallel", "parallel"),
        name="short_conv",
    )(p, p, p, w)


def _ffn(xn, w_gate, w_up, w_down, layer):
    pad = -w_gate.shape[2] % FF_ALIGN
    wg = cast_weight(w_gate, layer, pad_cols=pad)
    wu = cast_weight(w_up, layer, pad_cols=pad)
    wd = cast_weight(w_down, layer, pad_rows=pad)
    act = ff_up(xn, wg, wu)
    return matmul_ksplit(act, wd, k_steps=4)


def _mixer_even(xn_all, n_ctx, seq, w_in_all, w_out_all, conv_w, a_log, dt_bias, gdn_norm, lam_vecs, diff_norm,
                lambda_init):
    bsz, rows, d = xn_all.shape
    qk_w = GDN_HEADS * GDN_D
    c_ab = 3 * qk_w + qk_w
    n_ab = 4 * GDN_HEADS
    w_in = w_in_all[0]
    w_ab = jnp.pad(w_in[:, c_ab:c_ab + n_ab], ((0, 0), (0, LANES - n_ab))).astype(BF16)
    x2 = xn_all.reshape(bsz * rows, d)
    pa = matmul(x2, cast_weight(w_in_all, 0, n_cols=c_ab), BF16).reshape(bsz, rows, -1)
    pb = matmul(x2, w_in[:, c_ab + n_ab:].astype(BF16), BF16).reshape(bsz, rows, -1)
    ab = matmul(x2, w_ab, F32, tn=LANES).reshape(bsz, rows, LANES)
    qkv = gdn_prep(pa, conv_w, n_ctx)
    gates = gdn_gates(ab, a_log, dt_bias)
    o_f, o_b = gdn_scan(qkv, gates, n_ctx)
    wide = DIFF_HEADS * 2 * DIFF_D
    ya = gdn_out(o_f, o_b, pa, c_ab // wide - 1, n_ctx // ROW_TILE, gdn_norm)
    qr, kr = rope_qk(pb, n_ctx, seq)
    yb = diff_attention(qr, kr, pb, 2 * DIFF_HEADS, lam_vecs, diff_norm, lambda_init, n_ctx)
    y = matmul_cat(ya.reshape(bsz * seq, -1), yb.reshape(bsz * seq, -1), cast_weight(w_out_all, 0))
    return y.reshape(bsz, seq, d)


def _mixer_odd(xn, w_in_all, w_out_all, conf_dw, conf_dw_b, conf_ln_g, conf_ln_b, sc_w):
    bsz, seq, d = xn.shape
    p = matmul(xn.reshape(bsz * seq, d), cast_weight(w_in_all, 0), BF16).reshape(bsz, seq, -1)
    yc = ln_silu(conf_conv(p, conf_dw, conf_dw_b), conf_ln_g, conf_ln_b)
    yd = short_conv(p, 2 * (conf_dw.shape[1] // LANES), sc_w)
    y = matmul_cat(yc.reshape(bsz * seq, -1), yd.reshape(bsz * seq, -1), cast_weight(w_out_all, 0))
    return y.reshape(bsz, seq, d)


@jax.jit
def _forward(x, c, ctx, c_ctx, w_ada, b_ada, g_mix_pre, g_mix_post, g_ffn_pre, g_ffn_post,
             w_ff_gate, w_ff_up, w_ff_down, w_in_even, w_out_even, gdn_conv, gdn_a_log,
             gdn_dt_bias, gdn_norm, diff_lambda, diff_norm, w_in_odd, w_out_odd, conf_dw,
             conf_dw_b, conf_ln_g, conf_ln_b, sc_conv):
    bsz, seq, d = x.shape
    n_ctx = ctx.shape[1]
    depth = w_ada.shape[0]
    assert depth == 2, "the context stream is only carried through the first (even) layer"
    cond = jnp.concatenate([c, c_ctx[None, :], jnp.zeros((SUBLANES - bsz - 1, d), F32)], axis=0)
    mods = [adaln(cond, w_ada, b_ada, i).reshape(SUBLANES, 6, 1, d) for i in range(depth)]

    xn_all = prenorm_with_ctx(x, ctx, g_mix_pre[0], mods[0], bsz)
    y = _mixer_even(xn_all, n_ctx, seq, w_in_even, w_out_even, gdn_conv[0], gdn_a_log[0],
                    gdn_dt_bias[0], gdn_norm[0], diff_lambda[0], diff_norm[0], 0.8 - 0.6 * math.exp(-0.3 * 0))
    x, xn = postnorm(x, y, mods[0], 2, g_mix_post[0], (g_ffn_pre[0], mods[0], 3, 4))
    f = _ffn(xn.reshape(bsz * seq, d), w_ff_gate, w_ff_up, w_ff_down, 0).reshape(bsz, seq, d)
    x, xn = postnorm(x, f, mods[0], 5, g_ffn_post[0], (g_mix_pre[1], mods[1], 0, 1))

    y = _mixer_odd(xn, w_in_odd, w_out_odd, conf_dw[0], conf_dw_b[0], conf_ln_g[0], conf_ln_b[0],
                   sc_conv[0])
    x, xn = postnorm(x, y, mods[1], 2, g_mix_post[1], (g_ffn_pre[1], mods[1], 3, 4))
    f = _ffn(xn.reshape(bsz * seq, d), w_ff_gate, w_ff_up, w_ff_down, 1).reshape(bsz, seq, d)
    return postnorm(x, f, mods[1], 5, g_ffn_post[1])


def kernel(x, c, ctx, c_ctx, w_ada, b_ada, g_mix_pre, g_mix_post, g_ffn_pre, g_ffn_post, w_ff_gate, w_ff_up,
           w_ff_down, w_in_even, w_out_even, gdn_conv, gdn_a_log, gdn_dt_bias, gdn_norm, diff_lambda, diff_norm,
           w_in_odd, w_out_odd, conf_dw, conf_dw_b, conf_ln_g, conf_ln_b, sc_conv):
    return _forward(x, c, ctx, c_ctx, w_ada, b_ada, g_mix_pre, g_mix_post, g_ffn_pre, g_ffn_post, w_ff_gate,
                    w_ff_up, w_ff_down, w_in_even, w_out_even, gdn_conv, gdn_a_log, gdn_dt_bias, gdn_norm,
                    diff_lambda, diff_norm, w_in_odd, w_out_odd, conf_dw, conf_dw_b, conf_ln_g, conf_ln_b,
                    sc_conv)
```

```python
import functools
import math

import jax
import jax.numpy as jnp
from jax import lax
from jax.experimental import pallas as pl
from jax.experimental.pallas import tpu as pltpu

F32 = jnp.float32
BF16 = jnp.bfloat16

GRID_W = 64
GDN_HEADS = 16
GDN_D = 128
GDN_CHUNK = 64
DIFF_HEADS = 8
DIFF_D = 128
ROPE_BASE = 10000.0
CONF_K = 31
CONF_PAD = 16

LANES = 128
SUBLANES = 8
VMEM_LIMIT = 56 * 1024 * 1024
ROW_TILE = 256
MM_TILE = 1024
FF_TILE = 512
FF_ALIGN = 1024
ATTN_SUBTILES = 2


def _params(*sem):
    return pltpu.CompilerParams(dimension_semantics=sem, vmem_limit_bytes=VMEM_LIMIT)


def _sigmoid(x):
    return 1.0 / (1.0 + jnp.exp(-x))


def _silu(x):
    return x * _sigmoid(x)


def _dot(a, b):
    return jnp.dot(a, b, preferred_element_type=F32)


def _dot_nt(a, b):
    return lax.dot_general(a, b, (((1,), (1,)), ((), ())), preferred_element_type=F32)


def _dot_tn(a, b):
    return lax.dot_general(a, b, (((0,), (0,)), ((), ())), preferred_element_type=F32)


def _hi_lo(x):
    hi = x.astype(BF16)
    return hi, (x - hi.astype(F32)).astype(BF16)


def _split_lhs(a):
    hi, lo = _hi_lo(a)
    return jnp.concatenate([hi, hi, lo], axis=1)


def _split_rhs(b):
    hi, lo = _hi_lo(b)
    return jnp.concatenate([hi, lo, hi], axis=0)


def _adaln_kernel(c_ref, w_ref, b_ref, o_ref):
    s = _silu(c_ref[...]).astype(BF16)
    o_ref[...] = _dot(s, w_ref[...].astype(BF16)) + b_ref[...]


def adaln(cond, w, b, layer):
    rows, d = cond.shape
    depth, _, n = w.shape
    tn = 512
    return pl.pallas_call(
        _adaln_kernel,
        grid=(n // tn,),
        in_specs=[pl.BlockSpec((rows, d), lambda j: (0, 0)),
                  pl.BlockSpec((None, d, tn), lambda j: (layer, 0, j)),
                  pl.BlockSpec((None, 1, tn), lambda j: (layer, 0, j))],
        out_specs=pl.BlockSpec((rows, tn), lambda j: (0, j)),
        out_shape=jax.ShapeDtypeStruct((rows, n), F32),
        compiler_params=_params("parallel"),
        name="adaln",
    )(cond, w, b.reshape(depth, 1, n))


def _rms(x, eps):
    return x * lax.rsqrt(jnp.mean(x * x, axis=-1, keepdims=True) + eps)


def _prenorm_ctx_kernel(x_ref, c_ref, g_ref, sh_ref, sc_ref, o_ref):
    xv = jnp.where(pl.program_id(1) == 0, c_ref[...], x_ref[...])
    y = _rms(xv, 1e-6) * g_ref[...]
    o_ref[...] = (y * (1.0 + sc_ref[...]) + sh_ref[...]).astype(o_ref.dtype)


def prenorm_with_ctx(x, ctx, g, mods, ctx_row):
    bsz, seq, d = x.shape
    n_ctx = ctx.shape[1]
    assert n_ctx == ROW_TILE
    nt = (seq + n_ctx) // ROW_TILE
    mrow = lambda b, t: jnp.where(t == 0, ctx_row, b)
    return pl.pallas_call(
        _prenorm_ctx_kernel,
        grid=(bsz, nt),
        in_specs=[pl.BlockSpec((None, ROW_TILE, d), lambda b, t: (b, jnp.maximum(t - 1, 0), 0)),
                  pl.BlockSpec((None, ROW_TILE, d), lambda b, t: (b, 0, 0)),
                  pl.BlockSpec((1, d), lambda b, t: (0, 0)),
                  pl.BlockSpec((None, None, 1, d), lambda b, t: (mrow(b, t), 0, 0, 0)),
                  pl.BlockSpec((None, None, 1, d), lambda b, t: (mrow(b, t), 1, 0, 0))],
        out_specs=pl.BlockSpec((None, ROW_TILE, d), lambda b, t: (b, t, 0)),
        out_shape=jax.ShapeDtypeStruct((bsz, seq + n_ctx, d), BF16),
        compiler_params=_params("parallel", "parallel"),
        name="prenorm_ctx",
    )(x, ctx, g.reshape(1, d), mods, mods)


def _postnorm_kernel(x_ref, y_ref, gate_ref, gpost_ref, gpre_ref, sh_ref, sc_ref, xo_ref, no_ref):
    xn = x_ref[...] + gate_ref[...] * (_rms(y_ref[...], 1e-6) * gpost_ref[...])
    xo_ref[...] = xn
    h = _rms(xn, 1e-6) * gpre_ref[...]
    no_ref[...] = (h * (1.0 + sc_ref[...]) + sh_ref[...]).astype(no_ref.dtype)


def _postnorm_last_kernel(x_ref, y_ref, gate_ref, gpost_ref, xo_ref):
    xo_ref[...] = x_ref[...] + gate_ref[...] * (_rms(y_ref[...], 1e-6) * gpost_ref[...])


def postnorm(x, y, mods, gate_idx, g_post, nxt=None):
    bsz, seq, d = x.shape
    row = pl.BlockSpec((None, ROW_TILE, d), lambda b, t: (b, t, 0))
    vec = pl.BlockSpec((1, d), lambda b, t: (0, 0))
    mod = lambda k: pl.BlockSpec((None, None, 1, d), lambda b, t: (b, k, 0, 0))
    grid = (bsz, seq // ROW_TILE)
    if nxt is None:
        return pl.pallas_call(
            _postnorm_last_kernel, grid=grid,
            in_specs=[row, row, mod(gate_idx), vec],
            out_specs=row,
            out_shape=jax.ShapeDtypeStruct((bsz, seq, d), F32),
            compiler_params=_params("parallel", "parallel"),
            name="postnorm_last",
        )(x, y, mods, g_post.reshape(1, d))
    g_pre, mods_n, shift_idx, scale_idx = nxt
    return pl.pallas_call(
        _postnorm_kernel, grid=grid,
        in_specs=[row, row, mod(gate_idx), vec, vec, mod(shift_idx), mod(scale_idx)],
        out_specs=[row, row],
        out_shape=[jax.ShapeDtypeStruct((bsz, seq, d), F32), jax.ShapeDtypeStruct((bsz, seq, d), BF16)],
        compiler_params=_params("parallel", "parallel"),
        name="postnorm",
    )(x, y, mods, g_post.reshape(1, d), g_pre.reshape(1, d), mods_n, mods_n)


def _cast_pad_kernel(w_ref, o_ref, *, n_row_tiles):
    cols = w_ref.shape[1]

    @pl.when(pl.program_id(0) < n_row_tiles)
    def _():
        o_ref[:, :cols] = w_ref[...].astype(o_ref.dtype)
        if o_ref.shape[1] > cols:
            o_ref[:, cols:] = jnp.zeros((o_ref.shape[0], o_ref.shape[1] - cols), o_ref.dtype)

    @pl.when(pl.program_id(0) >= n_row_tiles)
    def _():
        o_ref[...] = jnp.zeros(o_ref.shape, o_ref.dtype)


def cast_weight(w, layer, n_cols=None, pad_rows=0, pad_cols=0):
    _, rows, cols = w.shape
    cols = cols if n_cols is None else n_cols
    assert rows % ROW_TILE == 0 and pad_rows % ROW_TILE == 0 and cols % LANES == 0 and pad_cols % LANES == 0
    n_row_tiles = rows // ROW_TILE
    kern = functools.partial(_cast_pad_kernel, n_row_tiles=n_row_tiles)
    return pl.pallas_call(
        kern,
        grid=((rows + pad_rows) // ROW_TILE,),
        in_specs=[pl.BlockSpec((None, ROW_TILE, cols), lambda i: (layer, jnp.minimum(i, n_row_tiles - 1), 0))],
        out_specs=pl.BlockSpec((ROW_TILE, cols + pad_cols), lambda i: (i, 0)),
        out_shape=jax.ShapeDtypeStruct((rows + pad_rows, cols + pad_cols), BF16),
        compiler_params=_params("parallel"),
        name="cast_weight",
    )(w)


def _mm_kernel(x_ref, w_ref, o_ref):
    o_ref[...] = _dot(x_ref[...], w_ref[...]).astype(o_ref.dtype)


def matmul(x, w, out_dtype, tn=MM_TILE, n_cols=None):
    m, k = x.shape
    n = w.shape[1] if n_cols is None else n_cols
    tm = MM_TILE
    return pl.pallas_call(
        _mm_kernel,
        grid=(m // tm, n // tn),
        in_specs=[pl.BlockSpec((tm, k), lambda i, j: (i, 0)),
                  pl.BlockSpec((k, tn), lambda i, j: (0, j))],
        out_specs=pl.BlockSpec((tm, tn), lambda i, j: (i, j)),
        out_shape=jax.ShapeDtypeStruct((m, n), out_dtype),
        compiler_params=_params("parallel", "parallel"),
        name="matmul",
    )(x, w)


def _mm2_kernel(a_ref, b_ref, w_ref, o_ref):
    ka = a_ref.shape[1]
    o_ref[...] = _dot(a_ref[...], w_ref[:ka, :]) + _dot(b_ref[...], w_ref[ka:, :])


def matmul_cat(a, b, w):
    m, ka = a.shape
    kb = b.shape[1]
    n = w.shape[1]
    tm = tn = MM_TILE
    return pl.pallas_call(
        _mm2_kernel,
        grid=(m // tm, n // tn),
        in_specs=[pl.BlockSpec((tm, ka), lambda i, j: (i, 0)),
                  pl.BlockSpec((tm, kb), lambda i, j: (i, 0)),
                  pl.BlockSpec((ka + kb, tn), lambda i, j: (0, j))],
        out_specs=pl.BlockSpec((tm, tn), lambda i, j: (i, j)),
        out_shape=jax.ShapeDtypeStruct((m, n), F32),
        compiler_params=_params("parallel", "parallel"),
        name="matmul_cat",
    )(a, b, w)


def _ff_up_kernel(x_ref, wg_ref, wu_ref, o_ref):
    x = x_ref[...]
    o_ref[...] = (_silu(_dot(x, wg_ref[...])) * _dot(x, wu_ref[...])).astype(o_ref.dtype)


def ff_up(x, wg, wu):
    m, k = x.shape
    n = wg.shape[1]
    tm, tn = MM_TILE, FF_TILE
    wspec = pl.BlockSpec((k, tn), lambda i, j: (0, j))
    return pl.pallas_call(
        _ff_up_kernel,
        grid=(m // tm, n // tn),
        in_specs=[pl.BlockSpec((tm, k), lambda i, j: (i, 0)), wspec, wspec],
        out_specs=pl.BlockSpec((tm, tn), lambda i, j: (i, j)),
        out_shape=jax.ShapeDtypeStruct((m, n), BF16),
        compiler_params=_params("parallel", "parallel"),
        name="ff_up",
    )(x, wg, wu)


def _mm_acc_kernel(x_ref, w_ref, o_ref):
    part = _dot(x_ref[...], w_ref[...])

    @pl.when(pl.program_id(2) == 0)
    def _():
        o_ref[...] = part

    @pl.when(pl.program_id(2) > 0)
    def _():
        o_ref[...] += part


def matmul_ksplit(x, w, k_steps):
    m, k = x.shape
    n = w.shape[1]
    tm = tn = MM_TILE
    tk = k // k_steps
    assert tk * k_steps == k and tk % LANES == 0
    return pl.pallas_call(
        _mm_acc_kernel,
        grid=(m // tm, n // tn, k_steps),
        in_specs=[pl.BlockSpec((tm, tk), lambda i, j, kk: (i, kk)),
                  pl.BlockSpec((tk, tn), lambda i, j, kk: (kk, j))],
        out_specs=pl.BlockSpec((tm, tn), lambda i, j, kk: (i, j)),
        out_shape=jax.ShapeDtypeStruct((m, n), F32),
        compiler_params=_params("parallel", "parallel", "arbitrary"),
        name="matmul_ksplit",
    )(x, w)


def _gdn_prep_kernel(p_ref, w_ref, o_ref, *, n_ctx, n_norm_blocks):
    x = p_ref[...].astype(F32)
    rows = x.shape[0]
    row = lax.broadcasted_iota(jnp.int32, x.shape, 0)
    prev = jnp.where((row == 0) | (row == n_ctx), 0.0, pltpu.roll(x, 1, 0))
    nxt = jnp.where((row == n_ctx - 1) | (row == rows - 1), 0.0, pltpu.roll(x, rows - 1, 0))
    w = w_ref[...]
    y = _silu(prev * w[0:1] + x * w[1:2] + nxt * w[2:3])
    yn = y * lax.rsqrt(jnp.sum(y * y, axis=-1, keepdims=True) + 1e-6)
    o_ref[...] = jnp.where(pl.program_id(1) < n_norm_blocks, yn, y)


def gdn_prep(p, conv_w, n_ctx):
    bsz, rows, _ = p.shape
    n_blocks = conv_w.shape[1] // LANES
    kern = functools.partial(_gdn_prep_kernel, n_ctx=n_ctx, n_norm_blocks=2 * GDN_HEADS)
    return pl.pallas_call(
        kern,
        grid=(bsz, n_blocks),
        in_specs=[pl.BlockSpec((None, rows, LANES), lambda b, j: (b, 0, j)),
                  pl.BlockSpec((conv_w.shape[0], LANES), lambda b, j: (0, j))],
        out_specs=pl.BlockSpec((None, rows, LANES), lambda b, j: (b, 0, j)),
        out_shape=jax.ShapeDtypeStruct((bsz, rows, n_blocks * LANES), F32),
        compiler_params=_params("parallel", "parallel"),
        name="gdn_prep",
    )(p, conv_w)


def _gdn_gates_kernel(ab_ref, alog_ref, dt_ref, o_ref):
    x = ab_ref[...]
    rows = x.shape[0]
    lane = lax.broadcasted_iota(jnp.int32, x.shape, 1)
    r64 = lax.broadcasted_iota(jnp.int32, x.shape, 0) & (GDN_CHUNK - 1)
    z = x + dt_ref[...]
    g = -jnp.exp(alog_ref[...]) * (jnp.maximum(z, 0.0) + jnp.log1p(jnp.exp(-jnp.abs(z))))
    pre = g
    suf = g
    s = 1
    while s < GDN_CHUNK:
        pre = pre + jnp.where(r64 >= s, pltpu.roll(pre, s, 0), 0.0)
        suf = suf + jnp.where(r64 < GDN_CHUNK - s, pltpu.roll(suf, rows - s, 0), 0.0)
        s *= 2
    val = jnp.where(lane < GDN_HEADS, pre, jnp.where(lane < 2 * GDN_HEADS, suf, _sigmoid(x)))
    for j in range(4 * GDN_HEADS):
        o_ref[j] = jnp.broadcast_to(val[:, j:j + 1], (rows, LANES))


def gdn_gates(ab, a_log, dt_bias):
    bsz, rows, _ = ab.shape
    pad = jnp.zeros((LANES - 2 * GDN_HEADS,), F32)
    alog = jnp.concatenate([a_log.reshape(-1), pad]).reshape(1, LANES)
    dt = jnp.concatenate([dt_bias.reshape(-1), pad]).reshape(1, LANES)
    return pl.pallas_call(
        _gdn_gates_kernel,
        grid=(bsz, rows // ROW_TILE),
        in_specs=[pl.BlockSpec((None, ROW_TILE, LANES), lambda b, t: (b, t, 0)),
                  pl.BlockSpec((1, LANES), lambda b, t: (0, 0)),
                  pl.BlockSpec((1, LANES), lambda b, t: (0, 0))],
        out_specs=pl.BlockSpec((None, 4 * GDN_HEADS, ROW_TILE, LANES), lambda b, t: (b, 0, t, 0)),
        out_shape=jax.ShapeDtypeStruct((bsz, 4 * GDN_HEADS, rows, LANES), F32),
        compiler_params=_params("parallel", "parallel"),
        name="gdn_gates",
    )(ab, alog, dt)


def _gdn_chains(chains):
    c = GDN_CHUNK
    n_chunks = chains[0][0].shape[0] // c
    r_iota = lax.broadcasted_iota(jnp.int32, (c, c), 0)
    c_iota = lax.broadcasted_iota(jnp.int32, (c, c), 1)
    eye = (r_iota == c_iota).astype(F32)
    masks = {True: (r_iota >= c_iota, r_iota > c_iota), False: (r_iota <= c_iota, r_iota < c_iota)}

    items = []
    for (q, k, v, g, beta, _, fwd) in chains:
        incl, strict = masks[fwd]
        for n in range(n_chunks):
            rs = slice(n * c, (n + 1) * c)
            qn = q[rs, :] * (GDN_D ** -0.5)
            kn, gn, bn = k[rs, :], g[rs, :], beta[rs, :]
            g_cols = gn.T[:c, :]
            decay = jnp.where(incl, jnp.exp(jnp.where(incl, gn[:, :c] - g_cols, 0.0)), 0.0)
            kb = kn * bn
            eg = jnp.exp(gn)
            g_last = gn[c - 1:c, :] if fwd else gn[0:1, :]
            items.append(dict(
                incl=incl, strict=strict, decay=decay, kbf=kn.astype(BF16),
                lhs=jnp.concatenate([kb, qn], axis=0).astype(BF16),
                rhs=jnp.concatenate([v[rs, :] * bn, kb * eg], axis=1),
                qd=qn * eg, kt=(kn * jnp.exp(g_last - gn)).astype(BF16), e_last=jnp.exp(g_last)))
    for it in items:
        it["both"] = _dot_nt(it["lhs"], it["kbf"])
    for it in items:
        lmat = jnp.where(it["strict"], it["both"][:c, :] * it["decay"], 0.0)
        it["qk"] = jnp.where(it["incl"], it["both"][c:, :] * it["decay"], 0.0).astype(BF16)
        it["tinv"] = eye - lmat
        it["pw"] = lmat
    for it in items:
        it["pw"] = _dot(_split_lhs(it["pw"]), _split_rhs(it["pw"]))
    for stage in range(5):
        for it in items:
            if stage < 4:
                prod = _dot(jnp.concatenate([_split_lhs(it["tinv"]), _split_lhs(it["pw"])], axis=0),
                            _split_rhs(it["pw"]))
                it["tinv"] = it["tinv"] + prod[:c, :]
                it["pw"] = prod[c:, :]
            else:
                it["tinv"] = it["tinv"] + _dot(_split_lhs(it["tinv"]), _split_rhs(it["pw"]))
    for it in items:
        sol = _dot(_split_lhs(it["tinv"]), _split_rhs(it["rhs"]))
        it["u"] = sol[:, :GDN_D]
        it["wq"] = jnp.concatenate([sol[:, GDN_D:], it["qd"]], axis=0).astype(BF16)

    states = [ch[5] for ch in chains]
    outs = [[None] * n_chunks for _ in chains]
    for step in range(n_chunks):
        cur = [items[ci * n_chunks + (step if ch[6] else n_chunks - 1 - step)] for ci, ch in enumerate(chains)]
        ws = [_dot(it["wq"], s.astype(BF16)) for it, s in zip(cur, states)]
        vb = [(it["u"] - w[:c, :]).astype(BF16) for it, w in zip(cur, ws)]
        for ci, (ch, it) in enumerate(zip(chains, cur)):
            n = step if ch[6] else n_chunks - 1 - step
            outs[ci][n] = ws[ci][c:, :] + _dot(it["qk"], vb[ci])
            states[ci] = states[ci] * it["e_last"] + _dot_tn(it["kt"], vb[ci])
    return [(jnp.concatenate(o, axis=0), s) for o, s in zip(outs, states)]


def _gdn_kernel(qf_ref, kf_ref, vf_ref, qb_ref, kb_ref, vb_ref, gf_ref, bf_ref, gb_ref, bb_ref,
                of_ref, ob_ref, s_ref):
    t = pl.program_id(2)

    @pl.when(t == 0)
    def _():
        s_ref[...] = jnp.zeros(s_ref.shape, F32)

    dirs = ((qf_ref, kf_ref, vf_ref, gf_ref, bf_ref, of_ref), (qb_ref, kb_ref, vb_ref, gb_ref, bb_ref, ob_ref))
    chains, dests = [], []
    for hh in range(gf_ref.shape[0]):
        cs = slice(hh * GDN_D, (hh + 1) * GDN_D)
        for d, (q_ref, k_ref, v_ref, g_ref, b_ref, o_ref) in enumerate(dirs):
            chains.append((q_ref[:, cs].astype(F32), k_ref[:, cs].astype(F32), v_ref[:, cs].astype(F32),
                           g_ref[hh], b_ref[hh], s_ref[2 * hh + d], d == 0))
            dests.append((o_ref, cs, 2 * hh + d))
    for (o_ref, cs, idx), (o, s_new) in zip(dests, _gdn_chains(chains)):
        s_ref[idx] = s_new
        o_ref[:, cs] = o


def gdn_scan(qkv, gates, n_ctx, heads_per_step=4):
    bsz, rows, _ = qkv.shape
    assert n_ctx == ROW_TILE
    h, hb = GDN_HEADS, heads_per_step
    nt = rows // ROW_TILE
    seq = rows - n_ctx
    wid = hb * GDN_D
    tile_f = lambda t: t
    tile_b = lambda t: jnp.where(t == 0, 0, nt - t)
    out_f = lambda t: jnp.maximum(t - 1, 0)
    out_b = lambda t: jnp.where(t == 0, nt - 2, nt - 1 - t)
    col = lambda part, tile: pl.BlockSpec((None, ROW_TILE, wid), lambda b, hg, t: (b, tile(t), part * (h // hb) + hg))
    gate = lambda kind, tile: pl.BlockSpec((None, hb, ROW_TILE, LANES),
                                           lambda b, hg, t: (b, kind * (h // hb) + hg, tile(t), 0))
    out = lambda tile: pl.BlockSpec((None, ROW_TILE, wid), lambda b, hg, t: (b, tile(t), hg))
    shp = jax.ShapeDtypeStruct((bsz, seq, h * GDN_D), F32)
    return pl.pallas_call(
        _gdn_kernel,
        grid=(bsz, h // hb, nt),
        in_specs=[col(0, tile_f), col(1, tile_f), col(2, tile_f), col(0, tile_b), col(1, tile_b), col(2, tile_b),
                  gate(0, tile_f), gate(2, tile_f), gate(1, tile_b), gate(3, tile_b)],
        out_specs=[out(out_f), out(out_b)],
        out_shape=[shp, shp],
        scratch_shapes=[pltpu.VMEM((2 * hb, GDN_D, GDN_D), F32)],
        compiler_params=_params("parallel", "parallel", "arbitrary"),
        name="gdn_scan",
    )(qkv, qkv, qkv, qkv, qkv, qkv, gates, gates, gates, gates)


def _gdn_out_kernel(of_ref, ob_ref, z_ref, gn_ref, y_ref):
    gn = gn_ref[...]
    for hh in range(GDN_HEADS):
        cs = slice(hh * GDN_D, (hh + 1) * GDN_D)
        o = of_ref[:, cs] + ob_ref[:, cs]
        y_ref[:, cs] = (_rms(o, 1e-6) * gn * _silu(z_ref[:, cs].astype(F32))).astype(y_ref.dtype)


def gdn_out(o_f, o_b, p, z_block, row_off_blocks, gn):
    bsz, seq, width = o_f.shape
    row = pl.BlockSpec((None, ROW_TILE, width), lambda b, t: (b, t, 0))
    return pl.pallas_call(
        _gdn_out_kernel,
        grid=(bsz, seq // ROW_TILE),
        in_specs=[row, row,
                  pl.BlockSpec((None, ROW_TILE, width), lambda b, t: (b, t + row_off_blocks, z_block)),
                  pl.BlockSpec((1, GDN_D), lambda b, t: (0, 0))],
        out_specs=row,
        out_shape=jax.ShapeDtypeStruct((bsz, seq, width), BF16),
        compiler_params=_params("parallel", "parallel"),
        name="gdn_out",
    )(o_f, o_b, p, gn.reshape(1, GDN_D))


def _rope_kernel(q_ref, k_ref, cos_ref, sa_ref, sb_ref, qo_ref, ko_ref, *, n_ctx_tiles):
    cos, sa, sb = cos_ref[...], sa_ref[...], sb_ref[...]

    def rope(x):
        return x * cos + pltpu.roll(x, LANES - DIFF_D // 4, 1) * sa + pltpu.roll(x, DIFF_D // 4, 1) * sb

    q_scale = (DIFF_D ** -0.5) * math.log2(math.e)
    for j in range(q_ref.shape[1] // DIFF_D):
        cs = slice(j * DIFF_D, (j + 1) * DIFF_D)
        ko_ref[:, cs] = rope(k_ref[:, cs].astype(F32)).astype(ko_ref.dtype)

    @pl.when(pl.program_id(1) >= n_ctx_tiles)
    def _():
        for j in range(q_ref.shape[1] // DIFF_D):
            cs = slice(j * DIFF_D, (j + 1) * DIFF_D)
            qo_ref[:, cs] = rope(q_ref[:, cs].astype(F32) * q_scale).astype(qo_ref.dtype)


def rope_tables(n_ctx, seq):
    half = DIFF_D // 2
    rows = seq // GRID_W
    inv = 1.0 / (ROPE_BASE ** (jnp.arange(0, half, 2, dtype=F32) / half))
    r = jnp.repeat(jnp.arange(rows), GRID_W).astype(F32)[:, None]
    col = jnp.tile(jnp.arange(GRID_W), rows).astype(F32)[:, None]
    ang = jnp.concatenate([r * inv, r * inv, col * inv, col * inv], axis=-1)
    cos, sin = jnp.cos(ang), jnp.sin(ang)
    first_quarter = (jnp.arange(DIFF_D) % half) < half // 2
    sin_a = jnp.where(first_quarter, -sin, 0.0)
    sin_b = jnp.where(first_quarter, 0.0, sin)
    ones = jnp.ones((n_ctx, DIFF_D), F32)
    zeros = jnp.zeros((n_ctx, DIFF_D), F32)
    cat = lambda a, b: jnp.concatenate([a, b], axis=0)
    return cat(ones, cos), cat(zeros, sin_a), cat(zeros, sin_b)


def rope_qk(p, n_ctx, seq):
    bsz, rows, _ = p.shape
    width = DIFF_HEADS * 2 * DIFF_D
    n_ctx_tiles = n_ctx // ROW_TILE
    cos, sa, sb = rope_tables(n_ctx, seq)
    blk = lambda off: pl.BlockSpec((None, ROW_TILE, width), lambda b, t: (b, t, off))
    tab = pl.BlockSpec((ROW_TILE, DIFF_D), lambda b, t: (t, 0))
    return pl.pallas_call(
        functools.partial(_rope_kernel, n_ctx_tiles=n_ctx_tiles),
        grid=(bsz, rows // ROW_TILE),
        in_specs=[blk(0), blk(1), tab, tab, tab],
        out_specs=[pl.BlockSpec((None, ROW_TILE, width), lambda b, t: (b, jnp.maximum(t - n_ctx_tiles, 0), 0)),
                   pl.BlockSpec((None, ROW_TILE, width), lambda b, t: (b, t, 0))],
        out_shape=[jax.ShapeDtypeStruct((bsz, seq, width), BF16), jax.ShapeDtypeStruct((bsz, rows, width), BF16)],
        compiler_params=_params("parallel", "arbitrary"),
        name="rope_qk",
    )(p, p, cos, sa, sb)


def _diff_attn_kernel(q_ref, k_ref, v_ref, lam_ref, dn_ref, o_ref, *, lambda_init):
    lv = lam_ref[...]
    lam = (jnp.exp(jnp.sum(lv[0:1] * lv[1:2], axis=-1, keepdims=True))
           - jnp.exp(jnp.sum(lv[2:3] * lv[3:4], axis=-1, keepdims=True)) + lambda_init)

    sub = q_ref.shape[0] // ATTN_SUBTILES
    tiles = [slice(i * sub, (i + 1) * sub) for i in range(ATTN_SUBTILES)]
    maps = [slice(m * DIFF_D, (m + 1) * DIFF_D) for m in range(2)]
    scores = [[_dot_nt(q_ref[rs, cs], k_ref[:, cs]) for cs in maps] for rs in tiles]
    weights = []
    for s0, s1 in scores:
        p0 = jnp.exp2(s0 - jnp.max(s0, axis=-1, keepdims=True))
        p1 = jnp.exp2(s1 - jnp.max(s1, axis=-1, keepdims=True))
        r0 = 1.0 / jnp.sum(p0, axis=-1, keepdims=True)
        r1 = 1.0 / jnp.sum(p1, axis=-1, keepdims=True)
        weights.append((p0 * r0 - p1 * (lam * r1)).astype(BF16))
    outs = [_dot(a, v_ref[...]) for a in weights]
    for rs, o in zip(tiles, outs):
        y = _rms(o, 1e-5) * dn_ref[...] * (1.0 - lambda_init)
        o_ref[rs, :] = y.astype(o_ref.dtype)


def diff_attention(q, k, v, v_block, lam_vecs, diff_norm, lambda_init, tq=512):
    bsz, seq, width = q.shape
    rows = k.shape[1]
    hw = 2 * DIFF_D
    kern = functools.partial(_diff_attn_kernel, lambda_init=lambda_init)
    kv = pl.BlockSpec((None, rows, hw), lambda b, h, i: (b, 0, h))
    return pl.pallas_call(
        kern,
        grid=(bsz, DIFF_HEADS, seq // tq),
        in_specs=[pl.BlockSpec((None, tq, hw), lambda b, h, i: (b, i, h)), kv,
                  pl.BlockSpec((None, rows, hw), lambda b, h, i: (b, 0, v_block + h)),
                  pl.BlockSpec((4, DIFF_D), lambda b, h, i: (0, 0)),
                  pl.BlockSpec((1, hw), lambda b, h, i: (0, 0))],
        out_specs=pl.BlockSpec((None, tq, hw), lambda b, h, i: (b, i, h)),
        out_shape=jax.ShapeDtypeStruct((bsz, seq, width), BF16),
        compiler_params=_params("parallel", "parallel", "parallel"),
        name="diff_attention",
    )(q, k, v, lam_vecs, diff_norm.reshape(1, hw))


def _conf_conv_kernel(a_ref, b_ref, w_ref, bias_ref, o_ref, buf_ref):
    rows = a_ref.shape[0]
    zeros = jnp.zeros((CONF_PAD, LANES), F32)
    buf_ref[0:CONF_PAD, :] = zeros
    buf_ref[CONF_PAD + rows:2 * CONF_PAD + rows, :] = zeros
    buf_ref[CONF_PAD:CONF_PAD + rows, :] = a_ref[...].astype(F32) * _sigmoid(b_ref[...].astype(F32))
    w = w_ref[...]
    bias = bias_ref[...]
    first = CONF_PAD - (CONF_K - 1) // 2

    def tile(t, carry):
        base = pl.multiple_of(t * ROW_TILE, ROW_TILE)
        acc = jnp.broadcast_to(bias, (ROW_TILE, LANES))
        for kk in range(CONF_K):
            acc = acc + w[kk:kk + 1, :] * buf_ref[pl.ds(base + first + kk, ROW_TILE), :]
        o_ref[pl.ds(base, ROW_TILE), :] = acc
        return carry

    lax.fori_loop(0, rows // ROW_TILE, tile, 0)


def conf_conv(p, dw, dw_b):
    bsz, rows, _ = p.shape
    ch = dw.shape[1]
    nb = ch // LANES
    return pl.pallas_call(
        _conf_conv_kernel,
        grid=(bsz, nb),
        in_specs=[pl.BlockSpec((None, rows, LANES), lambda b, j: (b, 0, j)),
                  pl.BlockSpec((None, rows, LANES), lambda b, j: (b, 0, nb + j)),
                  pl.BlockSpec((CONF_K, LANES), lambda b, j: (0, j)),
                  pl.BlockSpec((1, LANES), lambda b, j: (0, j))],
        out_specs=pl.BlockSpec((None, rows, LANES), lambda b, j: (b, 0, j)),
        out_shape=jax.ShapeDtypeStruct((bsz, rows, ch), F32),
        scratch_shapes=[pltpu.VMEM((rows + 2 * CONF_PAD, LANES), F32)],
        compiler_params=_params("parallel", "parallel"),
        name="conf_conv",
    )(p, p, dw, dw_b.reshape(1, ch))


def _ln_silu_kernel(x_ref, g_ref, b_ref, o_ref):
    x = x_ref[...]
    xc = x - jnp.mean(x, axis=-1, keepdims=True)
    y = xc * lax.rsqrt(jnp.mean(xc * xc, axis=-1, keepdims=True) + 1e-5) * g_ref[...] + b_ref[...]
    o_ref[...] = _silu(y).astype(o_ref.dtype)


def ln_silu(x, g, b):
    bsz, rows, ch = x.shape
    row = pl.BlockSpec((None, ROW_TILE, ch), lambda bb, t: (bb, t, 0))
    vec = pl.BlockSpec((1, ch), lambda bb, t: (0, 0))
    return pl.pallas_call(
        _ln_silu_kernel,
        grid=(bsz, rows // ROW_TILE),
        in_specs=[row, vec, vec],
        out_specs=row,
        out_shape=jax.ShapeDtypeStruct((bsz, rows, ch), BF16),
        compiler_params=_params("parallel", "parallel"),
        name="ln_silu",
    )(x, g.reshape(1, ch), b.reshape(1, ch))


def _short_conv_kernel(gb_ref, gc_ref, sh_ref, w_ref, o_ref):
    m = gc_ref[...].astype(F32) * sh_ref[...].astype(F32)
    rows = m.shape[0]
    row = lax.broadcasted_iota(jnp.int32, m.shape, 0)
    prev = jnp.where(row == 0, 0.0, pltpu.roll(m, 1, 0))
    nxt = jnp.where(row == rows - 1, 0.0, pltpu.roll(m, rows - 1, 0))
    w = w_ref[...]
    o_ref[...] = (gb_ref[...].astype(F32) * (prev * w[0:1] + m * w[1:2] + nxt * w[2:3])).astype(o_ref.dtype)


def short_conv(p, first_block, w):
    bsz, rows, _ = p.shape
    ch = w.shape[1]
    nb = ch // LANES
    blk = lambda off: pl.BlockSpec((None, rows, LANES), lambda b, j: (b, 0, first_block + off + j))
    return pl.pallas_call(
        _short_conv_kernel,
        grid=(bsz, nb),
        in_specs=[blk(0), blk(nb), blk(2 * nb), pl.BlockSpec((w.shape[0], LANES), lambda b, j: (0, j))],
        out_specs=pl.BlockSpec((None, rows, LANES), lambda b, j: (b, 0, j)),
        out_shape=jax.ShapeDtypeStruct((bsz, rows, ch), BF16),
        compiler_params=_params("parallel", "parallel"),
        name="short_conv",
    )(p, p, p, w)


def _ffn(xn, w_gate, w_up, w_down, layer):
    pad = -w_gate.shape[2] % FF_ALIGN
    wg = cast_weight(w_gate, layer, pad_cols=pad)
    wu = cast_weight(w_up, layer, pad_cols=pad)
    wd = cast_weight(w_down, layer, pad_rows=pad)
    act = ff_up(xn, wg, wu)
    return matmul_ksplit(act, wd, k_steps=4)


def _mixer_even(xn_all, n_ctx, seq, w_in_all, w_out_all, conv_w, a_log, dt_bias, gdn_norm, lam_vecs, diff_norm,
                lambda_init):
    bsz, rows, d = xn_all.shape
    qk_w = GDN_HEADS * GDN_D
    c_ab = 3 * qk_w + qk_w
    n_ab = 4 * GDN_HEADS
    w_in = w_in_all[0].astype(BF16)
    w_ab = jnp.pad(w_in[:, c_ab:c_ab + n_ab], ((0, 0), (0, LANES - n_ab)))
    x2 = xn_all.reshape(bsz * rows, d)
    pa = matmul(x2, w_in, BF16, n_cols=c_ab).reshape(bsz, rows, -1)
    pb = matmul(x2, w_in[:, c_ab + n_ab:], BF16).reshape(bsz, rows, -1)
    ab = matmul(x2, w_ab, F32, tn=LANES).reshape(bsz, rows, LANES)
    qkv = gdn_prep(pa, conv_w, n_ctx)
    gates = gdn_gates(ab, a_log, dt_bias)
    o_f, o_b = gdn_scan(qkv, gates, n_ctx)
    wide = DIFF_HEADS * 2 * DIFF_D
    ya = gdn_out(o_f, o_b, pa, c_ab // wide - 1, n_ctx // ROW_TILE, gdn_norm)
    qr, kr = rope_qk(pb, n_ctx, seq)
    yb = diff_attention(qr, kr, pb, 2 * DIFF_HEADS, lam_vecs, diff_norm, lambda_init)
    y = matmul_cat(ya.reshape(bsz * seq, -1), yb.reshape(bsz * seq, -1), cast_weight(w_out_all, 0))
    return y.reshape(bsz, seq, d)


def _mixer_odd(xn, w_in_all, w_out_all, conf_dw, conf_dw_b, conf_ln_g, conf_ln_b, sc_w):
    bsz, seq, d = xn.shape
    p = matmul(xn.reshape(bsz * seq, d), cast_weight(w_in_all, 0), BF16).reshape(bsz, seq, -1)
    yc = ln_silu(conf_conv(p, conf_dw, conf_dw_b), conf_ln_g, conf_ln_b)
    yd = short_conv(p, 2 * (conf_dw.shape[1] // LANES), sc_w)
    y = matmul_cat(yc.reshape(bsz * seq, -1), yd.reshape(bsz * seq, -1), cast_weight(w_out_all, 0))
    return y.reshape(bsz, seq, d)


@jax.jit
def _forward(x, c, ctx, c_ctx, w_ada, b_ada, g_mix_pre, g_mix_post, g_ffn_pre, g_ffn_post,
             w_ff_gate, w_ff_up, w_ff_down, w_in_even, w_out_even, gdn_conv, gdn_a_log,
             gdn_dt_bias, gdn_norm, diff_lambda, diff_norm, w_in_odd, w_out_odd, conf_dw,
             conf_dw_b, conf_ln_g, conf_ln_b, sc_conv):
    bsz, seq, d = x.shape
    n_ctx = ctx.shape[1]
    depth = w_ada.shape[0]
    assert depth == 2, "the context stream is only carried through the first (even) layer"
    cond = jnp.concatenate([c, c_ctx[None, :], jnp.zeros((SUBLANES - bsz - 1, d), F32)], axis=0)
    mods = [adaln(cond, w_ada, b_ada, i).reshape(SUBLANES, 6, 1, d) for i in range(depth)]

    xn_all = prenorm_with_ctx(x, ctx, g_mix_pre[0], mods[0], bsz)
    y = _mixer_even(xn_all, n_ctx, seq, w_in_even, w_out_even, gdn_conv[0], gdn_a_log[0],
                    gdn_dt_bias[0], gdn_norm[0], diff_lambda[0], diff_norm[0], 0.8 - 0.6 * math.exp(-0.3 * 0))
    x, xn = postnorm(x, y, mods[0], 2, g_mix_post[0], (g_ffn_pre[0], mods[0], 3, 4))
    f = _ffn(xn.reshape(bsz * seq, d), w_ff_gate, w_ff_up, w_ff_down, 0).reshape(bsz, seq, d)
    x, xn = postnorm(x, f, mods[0], 5, g_ffn_post[0], (g_mix_pre[1], mods[1], 0, 1))

    y = _mixer_odd(xn, w_in_odd, w_out_odd, conf_dw[0], conf_dw_b[0], conf_ln_g[0], conf_ln_b[0],
                   sc_conv[0])
    x, xn = postnorm(x, y, mods[1], 2, g_mix_post[1], (g_ffn_pre[1], mods[1], 3, 4))
    f = _ffn(xn.reshape(bsz * seq, d), w_ff_gate, w_ff_up, w_ff_down, 1).reshape(bsz, seq, d)
    return postnorm(x, f, mods[1], 5, g_ffn_post[1])


def kernel(x, c, ctx, c_ctx, w_ada, b_ada, g_mix_pre, g_mix_post, g_ffn_pre, g_ffn_post, w_ff_gate, w_ff_up,
           w_ff_down, w_in_even, w_out_even, gdn_conv, gdn_a_log, gdn_dt_bias, gdn_norm, diff_lambda, diff_norm,
           w_in_odd, w_out_odd, conf_dw, conf_dw_b, conf_ln_g, conf_ln_b, sc_conv):
    return _forward(x, c, ctx, c_ctx, w_ada, b_ada, g_mix_pre, g_mix_post, g_ffn_pre, g_ffn_post, w_ff_gate,
                    w_ff_up, w_ff_down, w_in_even, w_out_even, gdn_conv, gdn_a_log, gdn_dt_bias, gdn_norm,
                    diff_lambda, diff_norm, w_in_odd, w_out_odd, conf_dw, conf_dw_b, conf_ln_g, conf_ln_b,
                    sc_conv)
```

```python
import functools
import math

import jax
import jax.numpy as jnp
from jax import lax
from jax.experimental import pallas as pl
from jax.experimental.pallas import tpu as pltpu

F32 = jnp.float32
BF16 = jnp.bfloat16

GRID_W = 64
GDN_HEADS = 16
GDN_D = 128
GDN_CHUNK = 64
DIFF_HEADS = 8
DIFF_D = 128
ROPE_BASE = 10000.0
CONF_K = 31
CONF_PAD = 16

LANES = 128
SUBLANES = 8
VMEM_LIMIT = 56 * 1024 * 1024
ROW_TILE = 256
MM_TILE = 1024
FF_TILE = 512
FF_ALIGN = 1024
ATTN_SUBTILES = 2


def _params(*sem):
    return pltpu.CompilerParams(dimension_semantics=sem, vmem_limit_bytes=VMEM_LIMIT)


def _sigmoid(x):
    return 1.0 / (1.0 + jnp.exp(-x))


def _silu(x):
    return x * _sigmoid(x)


def _dot(a, b):
    return jnp.dot(a, b, preferred_element_type=F32)


def _dot_nt(a, b):
    return lax.dot_general(a, b, (((1,), (1,)), ((), ())), preferred_element_type=F32)


def _dot_tn(a, b):
    return lax.dot_general(a, b, (((0,), (0,)), ((), ())), preferred_element_type=F32)


def _hi_lo(x):
    hi = x.astype(BF16)
    return hi, (x - hi.astype(F32)).astype(BF16)


def _split_lhs(a):
    hi, lo = _hi_lo(a)
    return jnp.concatenate([hi, hi, lo], axis=1)


def _split_rhs(b):
    hi, lo = _hi_lo(b)
    return jnp.concatenate([hi, lo, hi], axis=0)


def _adaln_kernel(c_ref, w_ref, b_ref, o_ref):
    s = _silu(c_ref[...]).astype(BF16)
    o_ref[...] = _dot(s, w_ref[...].astype(BF16)) + b_ref[...]


def adaln(cond, w, b, layer):
    rows, d = cond.shape
    depth, _, n = w.shape
    tn = 512
    return pl.pallas_call(
        _adaln_kernel,
        grid=(n // tn,),
        in_specs=[pl.BlockSpec((rows, d), lambda j: (0, 0)),
                  pl.BlockSpec((None, d, tn), lambda j: (layer, 0, j)),
                  pl.BlockSpec((None, 1, tn), lambda j: (layer, 0, j))],
        out_specs=pl.BlockSpec((rows, tn), lambda j: (0, j)),
        out_shape=jax.ShapeDtypeStruct((rows, n), F32),
        compiler_params=_params("parallel"),
        name="adaln",
    )(cond, w, b.reshape(depth, 1, n))


def _rms(x, eps):
    return x * lax.rsqrt(jnp.mean(x * x, axis=-1, keepdims=True) + eps)


def _prenorm_ctx_kernel(x_ref, c_ref, g_ref, sh_ref, sc_ref, o_ref):
    xv = jnp.where(pl.program_id(1) == 0, c_ref[...], x_ref[...])
    y = _rms(xv, 1e-6) * g_ref[...]
    o_ref[...] = (y * (1.0 + sc_ref[...]) + sh_ref[...]).astype(o_ref.dtype)


def prenorm_with_ctx(x, ctx, g, mods, ctx_row):
    bsz, seq, d = x.shape
    n_ctx = ctx.shape[1]
    assert n_ctx == ROW_TILE
    nt = (seq + n_ctx) // ROW_TILE
    mrow = lambda b, t: jnp.where(t == 0, ctx_row, b)
    return pl.pallas_call(
        _prenorm_ctx_kernel,
        grid=(bsz, nt),
        in_specs=[pl.BlockSpec((None, ROW_TILE, d), lambda b, t: (b, jnp.maximum(t - 1, 0), 0)),
                  pl.BlockSpec((None, ROW_TILE, d), lambda b, t: (b, 0, 0)),
                  pl.BlockSpec((1, d), lambda b, t: (0, 0)),
                  pl.BlockSpec((None, None, 1, d), lambda b, t: (mrow(b, t), 0, 0, 0)),
                  pl.BlockSpec((None, None, 1, d), lambda b, t: (mrow(b, t), 1, 0, 0))],
        out_specs=pl.BlockSpec((None, ROW_TILE, d), lambda b, t: (b, t, 0)),
        out_shape=jax.ShapeDtypeStruct((bsz, seq + n_ctx, d), BF16),
        compiler_params=_params("parallel", "parallel"),
        name="prenorm_ctx",
    )(x, ctx, g.reshape(1, d), mods, mods)


def _postnorm_kernel(x_ref, y_ref, gate_ref, gpost_ref, gpre_ref, sh_ref, sc_ref, xo_ref, no_ref):
    xn = x_ref[...] + gate_ref[...] * (_rms(y_ref[...], 1e-6) * gpost_ref[...])
    xo_ref[...] = xn
    h = _rms(xn, 1e-6) * gpre_ref[...]
    no_ref[...] = (h * (1.0 + sc_ref[...]) + sh_ref[...]).astype(no_ref.dtype)


def _postnorm_last_kernel(x_ref, y_ref, gate_ref, gpost_ref, xo_ref):
    xo_ref[...] = x_ref[...] + gate_ref[...] * (_rms(y_ref[...], 1e-6) * gpost_ref[...])


def postnorm(x, y, mods, gate_idx, g_post, nxt=None, mod_row=None):
    bsz, seq, d = x.shape
    row = pl.BlockSpec((None, ROW_TILE, d), lambda b, t: (b, t, 0))
    vec = pl.BlockSpec((1, d), lambda b, t: (0, 0))
    mod = lambda k: pl.BlockSpec((None, None, 1, d), lambda b, t: (b if mod_row is None else mod_row, k, 0, 0))
    grid = (bsz, seq // ROW_TILE)
    if nxt is None:
        return pl.pallas_call(
            _postnorm_last_kernel, grid=grid,
            in_specs=[row, row, mod(gate_idx), vec],
            out_specs=row,
            out_shape=jax.ShapeDtypeStruct((bsz, seq, d), F32),
            compiler_params=_params("parallel", "parallel"),
            name="postnorm_last",
        )(x, y, mods, g_post.reshape(1, d))
    g_pre, mods_n, shift_idx, scale_idx = nxt
    return pl.pallas_call(
        _postnorm_kernel, grid=grid,
        in_specs=[row, row, mod(gate_idx), vec, vec, mod(shift_idx), mod(scale_idx)],
        out_specs=[row, row],
        out_shape=[jax.ShapeDtypeStruct((bsz, seq, d), F32), jax.ShapeDtypeStruct((bsz, seq, d), BF16)],
        compiler_params=_params("parallel", "parallel"),
        name="postnorm",
    )(x, y, mods, g_post.reshape(1, d), g_pre.reshape(1, d), mods_n, mods_n)


def _cast_pad_kernel(w_ref, o_ref, *, n_row_tiles):
    cols = w_ref.shape[1]

    @pl.when(pl.program_id(0) < n_row_tiles)
    def _():
        o_ref[:, :cols] = w_ref[...].astype(o_ref.dtype)
        if o_ref.shape[1] > cols:
            o_ref[:, cols:] = jnp.zeros((o_ref.shape[0], o_ref.shape[1] - cols), o_ref.dtype)

    @pl.when(pl.program_id(0) >= n_row_tiles)
    def _():
        o_ref[...] = jnp.zeros(o_ref.shape, o_ref.dtype)


def cast_weight(w, layer, n_cols=None, pad_rows=0, pad_cols=0):
    _, rows, cols = w.shape
    cols = cols if n_cols is None else n_cols
    assert rows % ROW_TILE == 0 and pad_rows % ROW_TILE == 0
    assert pad_cols == 0 or (cols % LANES == 0 and pad_cols % LANES == 0)
    n_row_tiles = rows // ROW_TILE
    kern = functools.partial(_cast_pad_kernel, n_row_tiles=n_row_tiles)
    return pl.pallas_call(
        kern,
        grid=((rows + pad_rows) // ROW_TILE,),
        in_specs=[pl.BlockSpec((None, ROW_TILE, cols), lambda i: (layer, jnp.minimum(i, n_row_tiles - 1), 0))],
        out_specs=pl.BlockSpec((ROW_TILE, cols + pad_cols), lambda i: (i, 0)),
        out_shape=jax.ShapeDtypeStruct((rows + pad_rows, cols + pad_cols), BF16),
        compiler_params=_params("parallel"),
        name="cast_weight",
    )(w)


def _mm_kernel(x_ref, w_ref, o_ref):
    o_ref[...] = _dot(x_ref[...], w_ref[...]).astype(o_ref.dtype)


def matmul(x, w, out_dtype, tn=MM_TILE, n_cols=None):
    m, k = x.shape
    n = w.shape[1] if n_cols is None else n_cols
    tm = MM_TILE
    return pl.pallas_call(
        _mm_kernel,
        grid=(m // tm, n // tn),
        in_specs=[pl.BlockSpec((tm, k), lambda i, j: (i, 0)),
                  pl.BlockSpec((k, tn), lambda i, j: (0, j))],
        out_specs=pl.BlockSpec((tm, tn), lambda i, j: (i, j)),
        out_shape=jax.ShapeDtypeStruct((m, n), out_dtype),
        compiler_params=_params("parallel", "parallel"),
        name="matmul",
    )(x, w)


def _mm2_kernel(a_ref, b_ref, w_ref, o_ref):
    ka = a_ref.shape[1]
    o_ref[...] = _dot(a_ref[...], w_ref[:ka, :]) + _dot(b_ref[...], w_ref[ka:, :])


def matmul_cat(a, b, w):
    m, ka = a.shape
    kb = b.shape[1]
    n = w.shape[1]
    tm = tn = MM_TILE
    return pl.pallas_call(
        _mm2_kernel,
        grid=(m // tm, n // tn),
        in_specs=[pl.BlockSpec((tm, ka), lambda i, j: (i, 0)),
                  pl.BlockSpec((tm, kb), lambda i, j: (i, 0)),
                  pl.BlockSpec((ka + kb, tn), lambda i, j: (0, j))],
        out_specs=pl.BlockSpec((tm, tn), lambda i, j: (i, j)),
        out_shape=jax.ShapeDtypeStruct((m, n), F32),
        compiler_params=_params("parallel", "parallel"),
        name="matmul_cat",
    )(a, b, w)


def _ff_up_kernel(x_ref, wg_ref, wu_ref, o_ref):
    x = x_ref[...]
    o_ref[...] = (_silu(_dot(x, wg_ref[...])) * _dot(x, wu_ref[...])).astype(o_ref.dtype)


def ff_up(x, wg, wu):
    m, k = x.shape
    n = wg.shape[1]
    tm, tn = MM_TILE, FF_TILE
    wspec = pl.BlockSpec((k, tn), lambda i, j: (0, j))
    return pl.pallas_call(
        _ff_up_kernel,
        grid=(m // tm, n // tn),
        in_specs=[pl.BlockSpec((tm, k), lambda i, j: (i, 0)), wspec, wspec],
        out_specs=pl.BlockSpec((tm, tn), lambda i, j: (i, j)),
        out_shape=jax.ShapeDtypeStruct((m, n), BF16),
        compiler_params=_params("parallel", "parallel"),
        name="ff_up",
    )(x, wg, wu)


def _mm_acc_kernel(x_ref, w_ref, o_ref):
    part = _dot(x_ref[...], w_ref[...])

    @pl.when(pl.program_id(2) == 0)
    def _():
        o_ref[...] = part

    @pl.when(pl.program_id(2) > 0)
    def _():
        o_ref[...] += part


def matmul_ksplit(x, w, k_steps):
    m, k = x.shape
    n = w.shape[1]
    tm = tn = MM_TILE
    tk = k // k_steps
    assert tk * k_steps == k and tk % LANES == 0
    return pl.pallas_call(
        _mm_acc_kernel,
        grid=(m // tm, n // tn, k_steps),
        in_specs=[pl.BlockSpec((tm, tk), lambda i, j, kk: (i, kk)),
                  pl.BlockSpec((tk, tn), lambda i, j, kk: (kk, j))],
        out_specs=pl.BlockSpec((tm, tn), lambda i, j, kk: (i, j)),
        out_shape=jax.ShapeDtypeStruct((m, n), F32),
        compiler_params=_params("parallel", "parallel", "arbitrary"),
        name="matmul_ksplit",
    )(x, w)


def _gdn_prep_kernel(p_ref, w_ref, o_ref, *, n_ctx, n_norm_blocks):
    x = p_ref[...].astype(F32)
    rows = x.shape[0]
    row = lax.broadcasted_iota(jnp.int32, x.shape, 0)
    prev = jnp.where((row == 0) | (row == n_ctx), 0.0, pltpu.roll(x, 1, 0))
    nxt = jnp.where((row == n_ctx - 1) | (row == rows - 1), 0.0, pltpu.roll(x, rows - 1, 0))
    w = w_ref[...]
    y = _silu(prev * w[0:1] + x * w[1:2] + nxt * w[2:3])
    yn = y * lax.rsqrt(jnp.sum(y * y, axis=-1, keepdims=True) + 1e-6)
    o_ref[...] = jnp.where(pl.program_id(1) < n_norm_blocks, yn, y)


def gdn_prep(p, conv_w, n_ctx):
    bsz, rows, _ = p.shape
    n_blocks = conv_w.shape[1] // LANES
    kern = functools.partial(_gdn_prep_kernel, n_ctx=n_ctx, n_norm_blocks=2 * GDN_HEADS)
    return pl.pallas_call(
        kern,
        grid=(bsz, n_blocks),
        in_specs=[pl.BlockSpec((None, rows, LANES), lambda b, j: (b, 0, j)),
                  pl.BlockSpec((conv_w.shape[0], LANES), lambda b, j: (0, j))],
        out_specs=pl.BlockSpec((None, rows, LANES), lambda b, j: (b, 0, j)),
        out_shape=jax.ShapeDtypeStruct((bsz, rows, n_blocks * LANES), F32),
        compiler_params=_params("parallel", "parallel"),
        name="gdn_prep",
    )(p, conv_w)


def _gdn_gates_kernel(ab_ref, alog_ref, dt_ref, o_ref):
    x = ab_ref[...]
    rows = x.shape[0]
    lane = lax.broadcasted_iota(jnp.int32, x.shape, 1)
    r64 = lax.broadcasted_iota(jnp.int32, x.shape, 0) & (GDN_CHUNK - 1)
    z = x + dt_ref[...]
    g = -jnp.exp(alog_ref[...]) * (jnp.maximum(z, 0.0) + jnp.log1p(jnp.exp(-jnp.abs(z))))
    pre = g
    suf = g
    s = 1
    while s < GDN_CHUNK:
        pre = pre + jnp.where(r64 >= s, pltpu.roll(pre, s, 0), 0.0)
        suf = suf + jnp.where(r64 < GDN_CHUNK - s, pltpu.roll(suf, rows - s, 0), 0.0)
        s *= 2
    val = jnp.where(lane < GDN_HEADS, pre, jnp.where(lane < 2 * GDN_HEADS, suf, _sigmoid(x)))
    for j in range(4 * GDN_HEADS):
        o_ref[j] = jnp.broadcast_to(val[:, j:j + 1], (rows, LANES))


def gdn_gates(ab, a_log, dt_bias):
    bsz, rows, _ = ab.shape
    pad = jnp.zeros((LANES - 2 * GDN_HEADS,), F32)
    alog = jnp.concatenate([a_log.reshape(-1), pad]).reshape(1, LANES)
    dt = jnp.concatenate([dt_bias.reshape(-1), pad]).reshape(1, LANES)
    return pl.pallas_call(
        _gdn_gates_kernel,
        grid=(bsz, rows // ROW_TILE),
        in_specs=[pl.BlockSpec((None, ROW_TILE, LANES), lambda b, t: (b, t, 0)),
                  pl.BlockSpec((1, LANES), lambda b, t: (0, 0)),
                  pl.BlockSpec((1, LANES), lambda b, t: (0, 0))],
        out_specs=pl.BlockSpec((None, 4 * GDN_HEADS, ROW_TILE, LANES), lambda b, t: (b, 0, t, 0)),
        out_shape=jax.ShapeDtypeStruct((bsz, 4 * GDN_HEADS, rows, LANES), F32),
        compiler_params=_params("parallel", "parallel"),
        name="gdn_gates",
    )(ab, alog, dt)


def _gdn_chains(chains):
    c = GDN_CHUNK
    n_chunks = chains[0][0].shape[0] // c
    r_iota = lax.broadcasted_iota(jnp.int32, (c, c), 0)
    c_iota = lax.broadcasted_iota(jnp.int32, (c, c), 1)
    eye = (r_iota == c_iota).astype(F32)
    masks = {True: (r_iota >= c_iota, r_iota > c_iota), False: (r_iota <= c_iota, r_iota < c_iota)}

    items = []
    for (q, k, v, g, beta, _, fwd) in chains:
        incl, strict = masks[fwd]
        for n in range(n_chunks):
            rs = slice(n * c, (n + 1) * c)
            qn = q[rs, :] * (GDN_D ** -0.5)
            kn, gn, bn = k[rs, :], g[rs, :], beta[rs, :]
            g_cols = gn.T[:c, :]
            decay = jnp.where(incl, jnp.exp(jnp.where(incl, gn[:, :c] - g_cols, 0.0)), 0.0)
            kb = kn * bn
            eg = jnp.exp(gn)
            g_last = gn[c - 1:c, :] if fwd else gn[0:1, :]
            items.append(dict(
                incl=incl, strict=strict, decay=decay, kbf=kn.astype(BF16),
                lhs=jnp.concatenate([kb, qn], axis=0).astype(BF16),
                rhs=jnp.concatenate([v[rs, :] * bn, kb * eg], axis=1),
                qd=qn * eg, kt=(kn * jnp.exp(g_last - gn)).astype(BF16), e_last=jnp.exp(g_last)))
    for it in items:
        it["both"] = _dot_nt(it["lhs"], it["kbf"])
    for it in items:
        lmat = jnp.where(it["strict"], it["both"][:c, :] * it["decay"], 0.0)
        it["qk"] = jnp.where(it["incl"], it["both"][c:, :] * it["decay"], 0.0).astype(BF16)
        it["tinv"] = eye - lmat
        it["pw"] = lmat
    for it in items:
        it["pw"] = _dot(_split_lhs(it["pw"]), _split_rhs(it["pw"]))
    for stage in range(5):
        for it in items:
            if stage < 4:
                prod = _dot(jnp.concatenate([_split_lhs(it["tinv"]), _split_lhs(it["pw"])], axis=0),
                            _split_rhs(it["pw"]))
                it["tinv"] = it["tinv"] + prod[:c, :]
                it["pw"] = prod[c:, :]
            else:
                it["tinv"] = it["tinv"] + _dot(_split_lhs(it["tinv"]), _split_rhs(it["pw"]))
    for it in items:
        sol = _dot(_split_lhs(it["tinv"]), _split_rhs(it["rhs"]))
        it["u"] = sol[:, :GDN_D]
        it["wq"] = jnp.concatenate([sol[:, GDN_D:], it["qd"]], axis=0).astype(BF16)

    states = [ch[5] for ch in chains]
    outs = [[None] * n_chunks for _ in chains]
    for step in range(n_chunks):
        cur = [items[ci * n_chunks + (step if ch[6] else n_chunks - 1 - step)] for ci, ch in enumerate(chains)]
        ws = [_dot(it["wq"], s.astype(BF16)) for it, s in zip(cur, states)]
        vb = [(it["u"] - w[:c, :]).astype(BF16) for it, w in zip(cur, ws)]
        for ci, (ch, it) in enumerate(zip(chains, cur)):
            n = step if ch[6] else n_chunks - 1 - step
            outs[ci][n] = ws[ci][c:, :] + _dot(it["qk"], vb[ci])
            states[ci] = states[ci] * it["e_last"] + _dot_tn(it["kt"], vb[ci])
    return [(jnp.concatenate(o, axis=0), s) for o, s in zip(outs, states)]


def _gdn_kernel(qf_ref, kf_ref, vf_ref, qb_ref, kb_ref, vb_ref, gf_ref, bf_ref, gb_ref, bb_ref,
                of_ref, ob_ref, cf_ref, cb_ref, s_ref):
    t = pl.program_id(2)

    @pl.when(t == 0)
    def _():
        s_ref[...] = jnp.zeros(s_ref.shape, F32)

    dirs = ((qf_ref, kf_ref, vf_ref, gf_ref, bf_ref, of_ref, cf_ref),
            (qb_ref, kb_ref, vb_ref, gb_ref, bb_ref, ob_ref, cb_ref))
    chains, dests = [], []
    for hh in range(gf_ref.shape[0]):
        cs = slice(hh * GDN_D, (hh + 1) * GDN_D)
        for d, (q_ref, k_ref, v_ref, g_ref, b_ref, o_ref, c_ref) in enumerate(dirs):
            chains.append((q_ref[:, cs].astype(F32), k_ref[:, cs].astype(F32), v_ref[:, cs].astype(F32),
                           g_ref[hh], b_ref[hh], s_ref[2 * hh + d], d == 0))
            dests.append((o_ref, c_ref, cs, 2 * hh + d))
    results = _gdn_chains(chains)
    for (_, _, _, idx), (_, s_new) in zip(dests, results):
        s_ref[idx] = s_new

    @pl.when(t == 0)
    def _():
        for (_, c_ref, cs, _), (o, _) in zip(dests, results):
            c_ref[:, cs] = o

    @pl.when(t > 0)
    def _():
        for (o_ref, _, cs, _), (o, _) in zip(dests, results):
            o_ref[:, cs] = o


def gdn_scan(qkv, gates, n_ctx, heads_per_step=4):
    bsz, rows, _ = qkv.shape
    assert n_ctx == ROW_TILE
    h, hb = GDN_HEADS, heads_per_step
    nt = rows // ROW_TILE
    seq = rows - n_ctx
    wid = hb * GDN_D
    tile_f = lambda t: t
    tile_b = lambda t: jnp.where(t == 0, 0, nt - t)
    out_f = lambda t: jnp.maximum(t - 1, 0)
    out_b = lambda t: jnp.where(t == 0, nt - 2, nt - 1 - t)
    col = lambda part, tile: pl.BlockSpec((None, ROW_TILE, wid), lambda b, hg, t: (b, tile(t), part * (h // hb) + hg))
    gate = lambda kind, tile: pl.BlockSpec((None, hb, ROW_TILE, LANES),
                                           lambda b, hg, t: (b, kind * (h // hb) + hg, tile(t), 0))
    out = lambda tile: pl.BlockSpec((None, ROW_TILE, wid), lambda b, hg, t: (b, tile(t), hg))
    shp = jax.ShapeDtypeStruct((bsz, seq, h * GDN_D), F32)
    shp_c = jax.ShapeDtypeStruct((bsz, n_ctx, h * GDN_D), F32)
    ctx_tile = lambda t: 0
    return pl.pallas_call(
        _gdn_kernel,
        grid=(bsz, h // hb, nt),
        in_specs=[col(0, tile_f), col(1, tile_f), col(2, tile_f), col(0, tile_b), col(1, tile_b), col(2, tile_b),
                  gate(0, tile_f), gate(2, tile_f), gate(1, tile_b), gate(3, tile_b)],
        out_specs=[out(out_f), out(out_b), out(ctx_tile), out(ctx_tile)],
        out_shape=[shp, shp, shp_c, shp_c],
        scratch_shapes=[pltpu.VMEM((2 * hb, GDN_D, GDN_D), F32)],
        compiler_params=_params("parallel", "parallel", "arbitrary"),
        name="gdn_scan",
    )(qkv, qkv, qkv, qkv, qkv, qkv, gates, gates, gates, gates)


def _gdn_out_kernel(of_ref, ob_ref, z_ref, gn_ref, y_ref):
    gn = gn_ref[...]
    for hh in range(GDN_HEADS):
        cs = slice(hh * GDN_D, (hh + 1) * GDN_D)
        o = of_ref[:, cs] + ob_ref[:, cs]
        y_ref[:, cs] = (_rms(o, 1e-6) * gn * _silu(z_ref[:, cs].astype(F32))).astype(y_ref.dtype)


def gdn_out(o_f, o_b, p, z_block, row_off_blocks, gn):
    bsz, seq, width = o_f.shape
    row = pl.BlockSpec((None, ROW_TILE, width), lambda b, t: (b, t, 0))
    return pl.pallas_call(
        _gdn_out_kernel,
        grid=(bsz, seq // ROW_TILE),
        in_specs=[row, row,
                  pl.BlockSpec((None, ROW_TILE, width), lambda b, t: (b, t + row_off_blocks, z_block)),
                  pl.BlockSpec((1, GDN_D), lambda b, t: (0, 0))],
        out_specs=row,
        out_shape=jax.ShapeDtypeStruct((bsz, seq, width), BF16),
        compiler_params=_params("parallel", "parallel"),
        name="gdn_out",
    )(o_f, o_b, p, gn.reshape(1, GDN_D))


def _rope_kernel(q_ref, k_ref, cos_ref, sa_ref, sb_ref, qo_ref, ko_ref, qc_ref, *, n_ctx_tiles):
    cos, sa, sb = cos_ref[...], sa_ref[...], sb_ref[...]

    def rope(x):
        return x * cos + pltpu.roll(x, LANES - DIFF_D // 4, 1) * sa + pltpu.roll(x, DIFF_D // 4, 1) * sb

    q_scale = (DIFF_D ** -0.5) * math.log2(math.e)
    for j in range(q_ref.shape[1] // DIFF_D):
        cs = slice(j * DIFF_D, (j + 1) * DIFF_D)
        ko_ref[:, cs] = rope(k_ref[:, cs].astype(F32)).astype(ko_ref.dtype)

    def write_q(dst_ref):
        for j in range(q_ref.shape[1] // DIFF_D):
            cs = slice(j * DIFF_D, (j + 1) * DIFF_D)
            dst_ref[:, cs] = rope(q_ref[:, cs].astype(F32) * q_scale).astype(dst_ref.dtype)

    pl.when(pl.program_id(1) >= n_ctx_tiles)(lambda: write_q(qo_ref))
    pl.when(pl.program_id(1) < n_ctx_tiles)(lambda: write_q(qc_ref))


def rope_tables(n_ctx, seq):
    half = DIFF_D // 2
    rows = seq // GRID_W
    inv = 1.0 / (ROPE_BASE ** (jnp.arange(0, half, 2, dtype=F32) / half))
    r = jnp.repeat(jnp.arange(rows), GRID_W).astype(F32)[:, None]
    col = jnp.tile(jnp.arange(GRID_W), rows).astype(F32)[:, None]
    ang = jnp.concatenate([r * inv, r * inv, col * inv, col * inv], axis=-1)
    cos, sin = jnp.cos(ang), jnp.sin(ang)
    first_quarter = (jnp.arange(DIFF_D) % half) < half // 2
    sin_a = jnp.where(first_quarter, -sin, 0.0)
    sin_b = jnp.where(first_quarter, 0.0, sin)
    ones = jnp.ones((n_ctx, DIFF_D), F32)
    zeros = jnp.zeros((n_ctx, DIFF_D), F32)
    cat = lambda a, b: jnp.concatenate([a, b], axis=0)
    return cat(ones, cos), cat(zeros, sin_a), cat(zeros, sin_b)


def rope_qk(p, n_ctx, seq):
    bsz, rows, _ = p.shape
    width = DIFF_HEADS * 2 * DIFF_D
    n_ctx_tiles = n_ctx // ROW_TILE
    cos, sa, sb = rope_tables(n_ctx, seq)
    blk = lambda off: pl.BlockSpec((None, ROW_TILE, width), lambda b, t: (b, t, off))
    tab = pl.BlockSpec((ROW_TILE, DIFF_D), lambda b, t: (t, 0))
    return pl.pallas_call(
        functools.partial(_rope_kernel, n_ctx_tiles=n_ctx_tiles),
        grid=(bsz, rows // ROW_TILE),
        in_specs=[blk(0), blk(1), tab, tab, tab],
        out_specs=[pl.BlockSpec((None, ROW_TILE, width), lambda b, t: (b, jnp.maximum(t - n_ctx_tiles, 0), 0)),
                   pl.BlockSpec((None, ROW_TILE, width), lambda b, t: (b, t, 0)),
                   pl.BlockSpec((None, ROW_TILE, width), lambda b, t: (b, jnp.minimum(t, n_ctx_tiles - 1), 0))],
        out_shape=[jax.ShapeDtypeStruct((bsz, seq, width), BF16), jax.ShapeDtypeStruct((bsz, rows, width), BF16),
                   jax.ShapeDtypeStruct((bsz, n_ctx, width), BF16)],
        compiler_params=_params("parallel", "arbitrary"),
        name="rope_qk",
    )(p, p, cos, sa, sb)


def _diff_attn_kernel(q_ref, k_ref, v_ref, lam_ref, dn_ref, o_ref, *, lambda_init):
    lv = lam_ref[...]
    lam = (jnp.exp(jnp.sum(lv[0:1] * lv[1:2], axis=-1, keepdims=True))
           - jnp.exp(jnp.sum(lv[2:3] * lv[3:4], axis=-1, keepdims=True)) + lambda_init)

    sub = q_ref.shape[0] // ATTN_SUBTILES
    tiles = [slice(i * sub, (i + 1) * sub) for i in range(ATTN_SUBTILES)]
    maps = [slice(m * DIFF_D, (m + 1) * DIFF_D) for m in range(2)]
    scores = [[_dot_nt(q_ref[rs, cs], k_ref[:, cs]) for cs in maps] for rs in tiles]
    weights = []
    for s0, s1 in scores:
        p0 = jnp.exp2(s0 - jnp.max(s0, axis=-1, keepdims=True))
        p1 = jnp.exp2(s1 - jnp.max(s1, axis=-1, keepdims=True))
        r0 = 1.0 / jnp.sum(p0, axis=-1, keepdims=True)
        r1 = 1.0 / jnp.sum(p1, axis=-1, keepdims=True)
        weights.append((p0 * r0 - p1 * (lam * r1)).astype(BF16))
    outs = [_dot(a, v_ref[...]) for a in weights]
    for rs, o in zip(tiles, outs):
        y = _rms(o, 1e-5) * dn_ref[...] * (1.0 - lambda_init)
        o_ref[rs, :] = y.astype(o_ref.dtype)


def diff_attention(q, k, v, v_block, lam_vecs, diff_norm, lambda_init, tq=512, kv_rows=None):
    bsz, seq, width = q.shape
    rows = k.shape[1] if kv_rows is None else kv_rows
    tq = min(tq, seq)
    hw = 2 * DIFF_D
    kern = functools.partial(_diff_attn_kernel, lambda_init=lambda_init)
    kv = pl.BlockSpec((None, rows, hw), lambda b, h, i: (b, 0, h))
    return pl.pallas_call(
        kern,
        grid=(bsz, DIFF_HEADS, seq // tq),
        in_specs=[pl.BlockSpec((None, tq, hw), lambda b, h, i: (b, i, h)), kv,
                  pl.BlockSpec((None, rows, hw), lambda b, h, i: (b, 0, v_block + h)),
                  pl.BlockSpec((4, DIFF_D), lambda b, h, i: (0, 0)),
                  pl.BlockSpec((1, hw), lambda b, h, i: (0, 0))],
        out_specs=pl.BlockSpec((None, tq, hw), lambda b, h, i: (b, i, h)),
        out_shape=jax.ShapeDtypeStruct((bsz, seq, width), BF16),
        compiler_params=_params("parallel", "parallel", "parallel"),
        name="diff_attention",
    )(q, k, v, lam_vecs, diff_norm.reshape(1, hw))


def _conf_conv_kernel(a_ref, b_ref, w_ref, bias_ref, o_ref, buf_ref):
    rows = a_ref.shape[0]
    zeros = jnp.zeros((CONF_PAD, LANES), F32)
    buf_ref[0:CONF_PAD, :] = zeros
    buf_ref[CONF_PAD + rows:2 * CONF_PAD + rows, :] = zeros
    buf_ref[CONF_PAD:CONF_PAD + rows, :] = a_ref[...].astype(F32) * _sigmoid(b_ref[...].astype(F32))
    w = w_ref[...]
    bias = bias_ref[...]
    first = CONF_PAD - (CONF_K - 1) // 2

    def tile(t, carry):
        base = pl.multiple_of(t * ROW_TILE, ROW_TILE)
        acc = jnp.broadcast_to(bias, (ROW_TILE, LANES))
        for kk in range(CONF_K):
            acc = acc + w[kk:kk + 1, :] * buf_ref[pl.ds(base + first + kk, ROW_TILE), :]
        o_ref[pl.ds(base, ROW_TILE), :] = acc
        return carry

    lax.fori_loop(0, rows // ROW_TILE, tile, 0)


def conf_conv(p, dw, dw_b):
    bsz, rows, _ = p.shape
    ch = dw.shape[1]
    nb = ch // LANES
    return pl.pallas_call(
        _conf_conv_kernel,
        grid=(bsz, nb),
        in_specs=[pl.BlockSpec((None, rows, LANES), lambda b, j: (b, 0, j)),
                  pl.BlockSpec((None, rows, LANES), lambda b, j: (b, 0, nb + j)),
                  pl.BlockSpec((CONF_K, LANES), lambda b, j: (0, j)),
                  pl.BlockSpec((1, LANES), lambda b, j: (0, j))],
        out_specs=pl.BlockSpec((None, rows, LANES), lambda b, j: (b, 0, j)),
        out_shape=jax.ShapeDtypeStruct((bsz, rows, ch), F32),
        scratch_shapes=[pltpu.VMEM((rows + 2 * CONF_PAD, LANES), F32)],
        compiler_params=_params("parallel", "parallel"),
        name="conf_conv",
    )(p, p, dw, dw_b.reshape(1, ch))


def _ln_silu_kernel(x_ref, g_ref, b_ref, o_ref):
    x = x_ref[...]
    xc = x - jnp.mean(x, axis=-1, keepdims=True)
    y = xc * lax.rsqrt(jnp.mean(xc * xc, axis=-1, keepdims=True) + 1e-5) * g_ref[...] + b_ref[...]
    o_ref[...] = _silu(y).astype(o_ref.dtype)


def ln_silu(x, g, b):
    bsz, rows, ch = x.shape
    row = pl.BlockSpec((None, ROW_TILE, ch), lambda bb, t: (bb, t, 0))
    vec = pl.BlockSpec((1, ch), lambda bb, t: (0, 0))
    return pl.pallas_call(
        _ln_silu_kernel,
        grid=(bsz, rows // ROW_TILE),
        in_specs=[row, vec, vec],
        out_specs=row,
        out_shape=jax.ShapeDtypeStruct((bsz, rows, ch), BF16),
        compiler_params=_params("parallel", "parallel"),
        name="ln_silu",
    )(x, g.reshape(1, ch), b.reshape(1, ch))


def _short_conv_kernel(gb_ref, gc_ref, sh_ref, w_ref, o_ref):
    m = gc_ref[...].astype(F32) * sh_ref[...].astype(F32)
    rows = m.shape[0]
    row = lax.broadcasted_iota(jnp.int32, m.shape, 0)
    prev = jnp.where(row == 0, 0.0, pltpu.roll(m, 1, 0))
    nxt = jnp.where(row == rows - 1, 0.0, pltpu.roll(m, rows - 1, 0))
    w = w_ref[...]
    o_ref[...] = (gb_ref[...].astype(F32) * (prev * w[0:1] + m * w[1:2] + nxt * w[2:3])).astype(o_ref.dtype)


def short_conv(p, first_block, w):
    bsz, rows, _ = p.shape
    ch = w.shape[1]
    nb = ch // LANES
    blk = lambda off: pl.BlockSpec((None, rows, LANES), lambda b, j: (b, 0, first_block + off + j))
    return pl.pallas_call(
        _short_conv_kernel,
        grid=(bsz, nb),
        in_specs=[blk(0), blk(nb), blk(2 * nb), pl.BlockSpec((w.shape[0], LANES), lambda b, j: (0, j))],
        out_specs=pl.BlockSpec((None, rows, LANES), lambda b, j: (b, 0, j)),
        out_shape=jax.ShapeDtypeStruct((bsz, rows, ch), BF16),
        compiler_params=_params("parallel", "parallel"),
        name="short_conv",
    )(p, p, p, w)


def _ffn(xn, w_gate, w_up, w_down, layer):
    pad = -w_gate.shape[2] % FF_ALIGN
    wg = cast_weight(w_gate, layer, pad_cols=pad)
    wu = cast_weight(w_up, layer, pad_cols=pad)
    wd = cast_weight(w_down, layer, pad_rows=pad)
    act = ff_up(xn, wg, wu)
    return matmul_ksplit(act, wd, k_steps=4)


def _mixer_even(xn_all, n_ctx, seq, w_in_all, w_out_all, conv_w, a_log, dt_bias, gdn_norm, lam_vecs, diff_norm,
                lambda_init):
    bsz, rows, d = xn_all.shape
    qk_w = GDN_HEADS * GDN_D
    c_ab = 3 * qk_w + qk_w
    n_ab = 4 * GDN_HEADS
    w_in = cast_weight(w_in_all, 0)
    w_ab = jnp.pad(w_in[:, c_ab:c_ab + n_ab], ((0, 0), (0, LANES - n_ab)))
    x2 = xn_all.reshape(bsz * rows, d)
    pa = matmul(x2, w_in, BF16, n_cols=c_ab).reshape(bsz, rows, -1)
    pb = matmul(x2, w_in[:, c_ab + n_ab:], BF16).reshape(bsz, rows, -1)
    ab = matmul(x2, w_ab, F32, tn=LANES).reshape(bsz, rows, LANES)
    qkv = gdn_prep(pa, conv_w, n_ctx)
    gates = gdn_gates(ab, a_log, dt_bias)
    o_f, o_b, c_f, c_b = gdn_scan(qkv, gates, n_ctx)
    z_block = c_ab // (DIFF_HEADS * 2 * DIFF_D) - 1
    ya = gdn_out(o_f, o_b, pa, z_block, n_ctx // ROW_TILE, gdn_norm)
    qr, kr, qc = rope_qk(pb, n_ctx, seq)
    yb = diff_attention(qr, kr, pb, 2 * DIFF_HEADS, lam_vecs, diff_norm, lambda_init)
    w_out = cast_weight(w_out_all, 0)
    y = matmul_cat(ya.reshape(bsz * seq, -1), yb.reshape(bsz * seq, -1), w_out)
    ya_c = gdn_out(c_f, c_b, pa, z_block, 0, gdn_norm)
    yb_c = diff_attention(qc, kr, pb, 2 * DIFF_HEADS, lam_vecs, diff_norm, lambda_init, kv_rows=n_ctx)
    y_c = matmul_cat(ya_c.reshape(bsz * n_ctx, -1), yb_c.reshape(bsz * n_ctx, -1), w_out)
    return y.reshape(bsz, seq, d), y_c.reshape(bsz, n_ctx, d)


def _mixer_odd(xn, w_in_all, w_out_all, conf_dw, conf_dw_b, conf_ln_g, conf_ln_b, sc_w):
    bsz, seq, d = xn.shape
    p = matmul(xn.reshape(bsz * seq, d), cast_weight(w_in_all, 0), BF16).reshape(bsz, seq, -1)
    yc = ln_silu(conf_conv(p, conf_dw, conf_dw_b), conf_ln_g, conf_ln_b)
    yd = short_conv(p, 2 * (conf_dw.shape[1] // LANES), sc_w)
    y = matmul_cat(yc.reshape(bsz * seq, -1), yd.reshape(bsz * seq, -1), cast_weight(w_out_all, 0))
    return y.reshape(bsz, seq, d)


@jax.jit
def _forward(x, c, ctx, c_ctx, w_ada, b_ada, g_mix_pre, g_mix_post, g_ffn_pre, g_ffn_post,
             w_ff_gate, w_ff_up, w_ff_down, w_in_even, w_out_even, gdn_conv, gdn_a_log,
             gdn_dt_bias, gdn_norm, diff_lambda, diff_norm, w_in_odd, w_out_odd, conf_dw,
             conf_dw_b, conf_ln_g, conf_ln_b, sc_conv):
    bsz, seq, d = x.shape
    n_ctx = ctx.shape[1]
    depth = w_ada.shape[0]
    assert depth == 2, "the context stream is only carried through the first (even) layer"
    cond = jnp.concatenate([c, c_ctx[None, :], jnp.zeros((SUBLANES - bsz - 1, d), F32)], axis=0)
    mods = [adaln(cond, w_ada, b_ada, i).reshape(SUBLANES, 6, 1, d) for i in range(depth)]

    xn_all = prenorm_with_ctx(x, ctx, g_mix_pre[0], mods[0], bsz)
    y, y_c = _mixer_even(xn_all, n_ctx, seq, w_in_even, w_out_even, gdn_conv[0], gdn_a_log[0],
                         gdn_dt_bias[0], gdn_norm[0], diff_lambda[0], diff_norm[0], 0.8 - 0.6 * math.exp(-0.3 * 0))
    x, xn = postnorm(x, y, mods[0], 2, g_mix_post[0], (g_ffn_pre[0], mods[0], 3, 4))
    f = _ffn(xn.reshape(bsz * seq, d), w_ff_gate, w_ff_up, w_ff_down, 0).reshape(bsz, seq, d)
    x, xn = postnorm(x, f, mods[0], 5, g_ffn_post[0], (g_mix_pre[1], mods[1], 0, 1))
    h, hn = postnorm(ctx, y_c, mods[0], 2, g_mix_post[0], (g_ffn_pre[0], mods[0], 3, 4), mod_row=bsz)
    f_c = _ffn(hn.reshape(bsz * n_ctx, d), w_ff_gate, w_ff_up, w_ff_down, 0).reshape(bsz, n_ctx, d)
    h = postnorm(h, f_c, mods[0], 5, g_ffn_post[0], mod_row=bsz)
    del h

    y = _mixer_odd(xn, w_in_odd, w_out_odd, conf_dw[0], conf_dw_b[0], conf_ln_g[0], conf_ln_b[0],
                   sc_conv[0])
    x, xn = postnorm(x, y, mods[1], 2, g_mix_post[1], (g_ffn_pre[1], mods[1], 3, 4))
    f = _ffn(xn.reshape(bsz * seq, d), w_ff_gate, w_ff_up, w_ff_down, 1).reshape(bsz, seq, d)
    return postnorm(x, f, mods[1], 5, g_ffn_post[1])


def kernel(x, c, ctx, c_ctx, w_ada, b_ada, g_mix_pre, g_mix_post, g_ffn_pre, g_ffn_post, w_ff_gate, w_ff_up,
           w_ff_down, w_in_even, w_out_even, gdn_conv, gdn_a_log, gdn_dt_bias, gdn_norm, diff_lambda, diff_norm,
           w_in_odd, w_out_odd, conf_dw, conf_dw_b, conf_ln_g, conf_ln_b, sc_conv):
    return _forward(x, c, ctx, c_ctx, w_ada, b_ada, g_mix_pre, g_mix_post, g_ffn_pre, g_ffn_post, w_ff_gate,
                    w_ff_up, w_ff_down, w_in_even, w_out_even, gdn_conv, gdn_a_log, gdn_dt_bias, gdn_norm,
                    diff_lambda, diff_norm, w_in_odd, w_out_odd, conf_dw, conf_dw_b, conf_ln_g, conf_ln_b,
                    sc_conv)
```

```python
import functools
import math

import jax
import jax.numpy as jnp
from jax import lax
from jax.experimental import pallas as pl
from jax.experimental.pallas import tpu as pltpu

F32 = jnp.float32
BF16 = jnp.bfloat16

GRID_W = 64
GDN_HEADS = 16
GDN_D = 128
GDN_CHUNK = 64
DIFF_HEADS = 8
DIFF_D = 128
ROPE_BASE = 10000.0
CONF_K = 31
CONF_PAD = 16

LANES = 128
SUBLANES = 8
VMEM_LIMIT = 56 * 1024 * 1024
ROW_TILE = 256
MM_TILE = 1024
FF_TILE = 512
FF_ALIGN = 1024
ATTN_SUBTILES = 2


def _params(*sem):
    return pltpu.CompilerParams(dimension_semantics=sem, vmem_limit_bytes=VMEM_LIMIT)


def _sigmoid(x):
    return 1.0 / (1.0 + jnp.exp(-x))


def _silu(x):
    return x * _sigmoid(x)


def _dot(a, b):
    return jnp.dot(a, b, preferred_element_type=F32)


def _dot_nt(a, b):
    return lax.dot_general(a, b, (((1,), (1,)), ((), ())), preferred_element_type=F32)


def _dot_tn(a, b):
    return lax.dot_general(a, b, (((0,), (0,)), ((), ())), preferred_element_type=F32)


def _hi_lo(x):
    hi = x.astype(BF16)
    return hi, (x - hi.astype(F32)).astype(BF16)


def _split_lhs(a):
    hi, lo = _hi_lo(a)
    return jnp.concatenate([hi, hi, lo], axis=1)


def _split_rhs(b):
    hi, lo = _hi_lo(b)
    return jnp.concatenate([hi, lo, hi], axis=0)


def _adaln_kernel(c_ref, w_ref, b_ref, o_ref):
    s = _silu(c_ref[...]).astype(BF16)
    o_ref[...] = _dot(s, w_ref[...].astype(BF16)) + b_ref[...]


def adaln(cond, w, b, layer):
    rows, d = cond.shape
    depth, _, n = w.shape
    tn = 512
    return pl.pallas_call(
        _adaln_kernel,
        grid=(n // tn,),
        in_specs=[pl.BlockSpec((rows, d), lambda j: (0, 0)),
                  pl.BlockSpec((None, d, tn), lambda j: (layer, 0, j)),
                  pl.BlockSpec((None, 1, tn), lambda j: (layer, 0, j))],
        out_specs=pl.BlockSpec((rows, tn), lambda j: (0, j)),
        out_shape=jax.ShapeDtypeStruct((rows, n), F32),
        compiler_params=_params("parallel"),
        name="adaln",
    )(cond, w, b.reshape(depth, 1, n))


def _rms(x, eps):
    return x * lax.rsqrt(jnp.mean(x * x, axis=-1, keepdims=True) + eps)


def _prenorm_ctx_kernel(x_ref, c_ref, g_ref, sh_ref, sc_ref, o_ref):
    xv = jnp.where(pl.program_id(1) == 0, c_ref[...], x_ref[...])
    y = _rms(xv, 1e-6) * g_ref[...]
    o_ref[...] = (y * (1.0 + sc_ref[...]) + sh_ref[...]).astype(o_ref.dtype)


def prenorm_with_ctx(x, ctx, g, mods, ctx_row):
    bsz, seq, d = x.shape
    n_ctx = ctx.shape[1]
    assert n_ctx == ROW_TILE
    nt = (seq + n_ctx) // ROW_TILE
    mrow = lambda b, t: jnp.where(t == 0, ctx_row, b)
    return pl.pallas_call(
        _prenorm_ctx_kernel,
        grid=(bsz, nt),
        in_specs=[pl.BlockSpec((None, ROW_TILE, d), lambda b, t: (b, jnp.maximum(t - 1, 0), 0)),
                  pl.BlockSpec((None, ROW_TILE, d), lambda b, t: (b, 0, 0)),
                  pl.BlockSpec((1, d), lambda b, t: (0, 0)),
                  pl.BlockSpec((None, None, 1, d), lambda b, t: (mrow(b, t), 0, 0, 0)),
                  pl.BlockSpec((None, None, 1, d), lambda b, t: (mrow(b, t), 1, 0, 0))],
        out_specs=pl.BlockSpec((None, ROW_TILE, d), lambda b, t: (b, t, 0)),
        out_shape=jax.ShapeDtypeStruct((bsz, seq + n_ctx, d), BF16),
        compiler_params=_params("parallel", "parallel"),
        name="prenorm_ctx",
    )(x, ctx, g.reshape(1, d), mods, mods)


def _postnorm_kernel(x_ref, y_ref, gate_ref, gpost_ref, gpre_ref, sh_ref, sc_ref, xo_ref, no_ref):
    xn = x_ref[...] + gate_ref[...] * (_rms(y_ref[...].astype(F32), 1e-6) * gpost_ref[...])
    xo_ref[...] = xn
    h = _rms(xn, 1e-6) * gpre_ref[...]
    no_ref[...] = (h * (1.0 + sc_ref[...]) + sh_ref[...]).astype(no_ref.dtype)


def _postnorm_last_kernel(x_ref, y_ref, gate_ref, gpost_ref, xo_ref):
    xo_ref[...] = x_ref[...] + gate_ref[...] * (_rms(y_ref[...].astype(F32), 1e-6) * gpost_ref[...])


def postnorm(x, y, mods, gate_idx, g_post, nxt=None, mod_row=None):
    bsz, seq, d = x.shape
    row = pl.BlockSpec((None, ROW_TILE, d), lambda b, t: (b, t, 0))
    vec = pl.BlockSpec((1, d), lambda b, t: (0, 0))
    mod = lambda k: pl.BlockSpec((None, None, 1, d), lambda b, t: (b if mod_row is None else mod_row, k, 0, 0))
    grid = (bsz, seq // ROW_TILE)
    if nxt is None:
        return pl.pallas_call(
            _postnorm_last_kernel, grid=grid,
            in_specs=[row, row, mod(gate_idx), vec],
            out_specs=row,
            out_shape=jax.ShapeDtypeStruct((bsz, seq, d), F32),
            compiler_params=_params("parallel", "parallel"),
            name="postnorm_last",
        )(x, y, mods, g_post.reshape(1, d))
    g_pre, mods_n, shift_idx, scale_idx = nxt
    return pl.pallas_call(
        _postnorm_kernel, grid=grid,
        in_specs=[row, row, mod(gate_idx), vec, vec, mod(shift_idx), mod(scale_idx)],
        out_specs=[row, row],
        out_shape=[jax.ShapeDtypeStruct((bsz, seq, d), F32), jax.ShapeDtypeStruct((bsz, seq, d), BF16)],
        compiler_params=_params("parallel", "parallel"),
        name="postnorm",
    )(x, y, mods, g_post.reshape(1, d), g_pre.reshape(1, d), mods_n, mods_n)


def _cast_pad_kernel(w_ref, o_ref, *, n_row_tiles):
    cols = w_ref.shape[1]

    @pl.when(pl.program_id(0) < n_row_tiles)
    def _():
        o_ref[:, :cols] = w_ref[...].astype(o_ref.dtype)
        if o_ref.shape[1] > cols:
            o_ref[:, cols:] = jnp.zeros((o_ref.shape[0], o_ref.shape[1] - cols), o_ref.dtype)

    @pl.when(pl.program_id(0) >= n_row_tiles)
    def _():
        o_ref[...] = jnp.zeros(o_ref.shape, o_ref.dtype)


def cast_weight(w, layer, n_cols=None, pad_rows=0, pad_cols=0):
    _, rows, cols = w.shape
    cols = cols if n_cols is None else n_cols
    assert rows % ROW_TILE == 0 and pad_rows % ROW_TILE == 0
    assert pad_cols == 0 or (cols % LANES == 0 and pad_cols % LANES == 0)
    n_row_tiles = rows // ROW_TILE
    kern = functools.partial(_cast_pad_kernel, n_row_tiles=n_row_tiles)
    return pl.pallas_call(
        kern,
        grid=((rows + pad_rows) // ROW_TILE,),
        in_specs=[pl.BlockSpec((None, ROW_TILE, cols), lambda i: (layer, jnp.minimum(i, n_row_tiles - 1), 0))],
        out_specs=pl.BlockSpec((ROW_TILE, cols + pad_cols), lambda i: (i, 0)),
        out_shape=jax.ShapeDtypeStruct((rows + pad_rows, cols + pad_cols), BF16),
        compiler_params=_params("parallel"),
        name="cast_weight",
    )(w)


def _mm_kernel(x_ref, w_ref, o_ref):
    o_ref[...] = _dot(x_ref[...], w_ref[...]).astype(o_ref.dtype)


def matmul(x, w, out_dtype, tn=MM_TILE, n_cols=None):
    m, k = x.shape
    n = w.shape[1] if n_cols is None else n_cols
    tm = MM_TILE
    return pl.pallas_call(
        _mm_kernel,
        grid=(m // tm, n // tn),
        in_specs=[pl.BlockSpec((tm, k), lambda i, j: (i, 0)),
                  pl.BlockSpec((k, tn), lambda i, j: (0, j))],
        out_specs=pl.BlockSpec((tm, tn), lambda i, j: (i, j)),
        out_shape=jax.ShapeDtypeStruct((m, n), out_dtype),
        compiler_params=_params("parallel", "parallel"),
        name="matmul",
    )(x, w)


def _mm2_kernel(a_ref, b_ref, w_ref, o_ref):
    ka = a_ref.shape[1]
    o_ref[...] = (_dot(a_ref[...], w_ref[:ka, :]) + _dot(b_ref[...], w_ref[ka:, :])).astype(o_ref.dtype)


def matmul_cat(a, b, w):
    m, ka = a.shape
    kb = b.shape[1]
    n = w.shape[1]
    tm = tn = MM_TILE
    return pl.pallas_call(
        _mm2_kernel,
        grid=(m // tm, n // tn),
        in_specs=[pl.BlockSpec((tm, ka), lambda i, j: (i, 0)),
                  pl.BlockSpec((tm, kb), lambda i, j: (i, 0)),
                  pl.BlockSpec((ka + kb, tn), lambda i, j: (0, j))],
        out_specs=pl.BlockSpec((tm, tn), lambda i, j: (i, j)),
        out_shape=jax.ShapeDtypeStruct((m, n), BF16),
        compiler_params=_params("parallel", "parallel"),
        name="matmul_cat",
    )(a, b, w)


def _ff_up_kernel(x_ref, wg_ref, wu_ref, o_ref):
    x = x_ref[...]
    o_ref[...] = (_silu(_dot(x, wg_ref[...])) * _dot(x, wu_ref[...])).astype(o_ref.dtype)


def ff_up(x, wg, wu):
    m, k = x.shape
    n = wg.shape[1]
    tm, tn = MM_TILE, FF_TILE
    wspec = pl.BlockSpec((k, tn), lambda i, j: (0, j))
    return pl.pallas_call(
        _ff_up_kernel,
        grid=(m // tm, n // tn),
        in_specs=[pl.BlockSpec((tm, k), lambda i, j: (i, 0)), wspec, wspec],
        out_specs=pl.BlockSpec((tm, tn), lambda i, j: (i, j)),
        out_shape=jax.ShapeDtypeStruct((m, n), BF16),
        compiler_params=_params("parallel", "parallel"),
        name="ff_up",
    )(x, wg, wu)


def _mm_acc_kernel(x_ref, w_ref, o_ref, acc_ref):
    part = _dot(x_ref[...], w_ref[...])
    kk = pl.program_id(2)
    last = pl.num_programs(2) - 1

    @pl.when(kk == 0)
    def _():
        acc_ref[...] = part

    @pl.when((kk > 0) & (kk < last))
    def _():
        acc_ref[...] += part

    @pl.when(kk == last)
    def _():
        o_ref[...] = (acc_ref[...] + part).astype(o_ref.dtype)


def matmul_ksplit(x, w, k_steps, tm=2 * MM_TILE):
    m, k = x.shape
    n = w.shape[1]
    tm = min(tm, m)
    tn = MM_TILE
    tk = k // k_steps
    assert k_steps >= 2 and tk * k_steps == k and tk % LANES == 0
    return pl.pallas_call(
        _mm_acc_kernel,
        grid=(m // tm, n // tn, k_steps),
        in_specs=[pl.BlockSpec((tm, tk), lambda i, j, kk: (i, kk)),
                  pl.BlockSpec((tk, tn), lambda i, j, kk: (kk, j))],
        out_specs=pl.BlockSpec((tm, tn), lambda i, j, kk: (i, j)),
        out_shape=jax.ShapeDtypeStruct((m, n), BF16),
        scratch_shapes=[pltpu.VMEM((tm, tn), F32)],
        compiler_params=_params("parallel", "parallel", "arbitrary"),
        name="matmul_ksplit",
    )(x, w)


def _gdn_prep_kernel(p_ref, w_ref, o_ref, *, n_ctx, n_norm_blocks):
    x = p_ref[...].astype(F32)
    rows = x.shape[0]
    row = lax.broadcasted_iota(jnp.int32, x.shape, 0)
    prev = jnp.where((row == 0) | (row == n_ctx), 0.0, pltpu.roll(x, 1, 0))
    nxt = jnp.where((row == n_ctx - 1) | (row == rows - 1), 0.0, pltpu.roll(x, rows - 1, 0))
    w = w_ref[...]
    y = _silu(prev * w[0:1] + x * w[1:2] + nxt * w[2:3])
    yn = y * lax.rsqrt(jnp.sum(y * y, axis=-1, keepdims=True) + 1e-6)
    o_ref[...] = jnp.where(pl.program_id(1) < n_norm_blocks, yn, y)


def gdn_prep(p, conv_w, n_ctx):
    bsz, rows, _ = p.shape
    n_blocks = conv_w.shape[1] // LANES
    kern = functools.partial(_gdn_prep_kernel, n_ctx=n_ctx, n_norm_blocks=2 * GDN_HEADS)
    return pl.pallas_call(
        kern,
        grid=(bsz, n_blocks),
        in_specs=[pl.BlockSpec((None, rows, LANES), lambda b, j: (b, 0, j)),
                  pl.BlockSpec((conv_w.shape[0], LANES), lambda b, j: (0, j))],
        out_specs=pl.BlockSpec((None, rows, LANES), lambda b, j: (b, 0, j)),
        out_shape=jax.ShapeDtypeStruct((bsz, rows, n_blocks * LANES), F32),
        compiler_params=_params("parallel", "parallel"),
        name="gdn_prep",
    )(p, conv_w)


def _gdn_gates_kernel(ab_ref, alog_ref, dt_ref, o_ref):
    x = ab_ref[...]
    rows = x.shape[0]
    lane = lax.broadcasted_iota(jnp.int32, x.shape, 1)
    r64 = lax.broadcasted_iota(jnp.int32, x.shape, 0) & (GDN_CHUNK - 1)
    z = x + dt_ref[...]
    g = -jnp.exp(alog_ref[...]) * (jnp.maximum(z, 0.0) + jnp.log1p(jnp.exp(-jnp.abs(z))))
    pre = g
    suf = g
    s = 1
    while s < GDN_CHUNK:
        pre = pre + jnp.where(r64 >= s, pltpu.roll(pre, s, 0), 0.0)
        suf = suf + jnp.where(r64 < GDN_CHUNK - s, pltpu.roll(suf, rows - s, 0), 0.0)
        s *= 2
    val = jnp.where(lane < GDN_HEADS, pre, jnp.where(lane < 2 * GDN_HEADS, suf, _sigmoid(x)))
    for j in range(4 * GDN_HEADS):
        o_ref[j] = jnp.broadcast_to(val[:, j:j + 1], (rows, LANES))


def gdn_gates(ab, a_log, dt_bias):
    bsz, rows, _ = ab.shape
    pad = jnp.zeros((LANES - 2 * GDN_HEADS,), F32)
    alog = jnp.concatenate([a_log.reshape(-1), pad]).reshape(1, LANES)
    dt = jnp.concatenate([dt_bias.reshape(-1), pad]).reshape(1, LANES)
    return pl.pallas_call(
        _gdn_gates_kernel,
        grid=(bsz, rows // ROW_TILE),
        in_specs=[pl.BlockSpec((None, ROW_TILE, LANES), lambda b, t: (b, t, 0)),
                  pl.BlockSpec((1, LANES), lambda b, t: (0, 0)),
                  pl.BlockSpec((1, LANES), lambda b, t: (0, 0))],
        out_specs=pl.BlockSpec((None, 4 * GDN_HEADS, ROW_TILE, LANES), lambda b, t: (b, 0, t, 0)),
        out_shape=jax.ShapeDtypeStruct((bsz, 4 * GDN_HEADS, rows, LANES), F32),
        compiler_params=_params("parallel", "parallel"),
        name="gdn_gates",
    )(ab, alog, dt)


def _gdn_chains(chains):
    c = GDN_CHUNK
    n_chunks = chains[0][0].shape[0] // c
    r_iota = lax.broadcasted_iota(jnp.int32, (c, c), 0)
    c_iota = lax.broadcasted_iota(jnp.int32, (c, c), 1)
    eye = (r_iota == c_iota).astype(F32)
    masks = {True: (r_iota >= c_iota, r_iota > c_iota), False: (r_iota <= c_iota, r_iota < c_iota)}

    items = []
    for (q, k, v, g, beta, _, fwd) in chains:
        incl, strict = masks[fwd]
        for n in range(n_chunks):
            rs = slice(n * c, (n + 1) * c)
            qn = q[rs, :] * (GDN_D ** -0.5)
            kn, gn, bn = k[rs, :], g[rs, :], beta[rs, :]
            g_cols = gn.T[:c, :]
            decay = jnp.where(incl, jnp.exp(jnp.where(incl, gn[:, :c] - g_cols, 0.0)), 0.0)
            kb = kn * bn
            eg = jnp.exp(gn)
            g_last = gn[c - 1:c, :] if fwd else gn[0:1, :]
            items.append(dict(
                incl=incl, strict=strict, decay=decay, kbf=kn.astype(BF16),
                lhs=jnp.concatenate([kb, qn], axis=0).astype(BF16),
                rhs=jnp.concatenate([v[rs, :] * bn, kb * eg], axis=1),
                qd=qn * eg, kt=(kn * jnp.exp(g_last - gn)).astype(BF16), e_last=jnp.exp(g_last)))
    for it in items:
        it["both"] = _dot_nt(it["lhs"], it["kbf"])
    for it in items:
        lmat = jnp.where(it["strict"], it["both"][:c, :] * it["decay"], 0.0)
        it["qk"] = jnp.where(it["incl"], it["both"][c:, :] * it["decay"], 0.0).astype(BF16)
        it["tinv"] = eye - lmat
        it["pw"] = lmat
    for it in items:
        it["pw"] = _dot(_split_lhs(it["pw"]), _split_rhs(it["pw"]))
    for stage in range(5):
        for it in items:
            if stage < 4:
                prod = _dot(jnp.concatenate([_split_lhs(it["tinv"]), _split_lhs(it["pw"])], axis=0),
                            _split_rhs(it["pw"]))
                it["tinv"] = it["tinv"] + prod[:c, :]
                it["pw"] = prod[c:, :]
            else:
                it["tinv"] = it["tinv"] + _dot(_split_lhs(it["tinv"]), _split_rhs(it["pw"]))
    for it in items:
        sol = _dot(_split_lhs(it["tinv"]), _split_rhs(it["rhs"]))
        it["u"] = sol[:, :GDN_D]
        it["wq"] = jnp.concatenate([sol[:, GDN_D:], it["qd"]], axis=0).astype(BF16)

    states = [ch[5] for ch in chains]
    outs = [[None] * n_chunks for _ in chains]
    for step in range(n_chunks):
        cur = [items[ci * n_chunks + (step if ch[6] else n_chunks - 1 - step)] for ci, ch in enumerate(chains)]
        ws = [_dot(it["wq"], s.astype(BF16)) for it, s in zip(cur, states)]
        vb = [(it["u"] - w[:c, :]).astype(BF16) for it, w in zip(cur, ws)]
        for ci, (ch, it) in enumerate(zip(chains, cur)):
            n = step if ch[6] else n_chunks - 1 - step
            outs[ci][n] = ws[ci][c:, :] + _dot(it["qk"], vb[ci])
            states[ci] = states[ci] * it["e_last"] + _dot_tn(it["kt"], vb[ci])
    return [(jnp.concatenate(o, axis=0), s) for o, s in zip(outs, states)]


def _gdn_kernel(qf_ref, kf_ref, vf_ref, qb_ref, kb_ref, vb_ref, gf_ref, bf_ref, gb_ref, bb_ref,
                of_ref, ob_ref, cf_ref, cb_ref, s_ref):
    t = pl.program_id(2)

    @pl.when(t == 0)
    def _():
        s_ref[...] = jnp.zeros(s_ref.shape, F32)

    dirs = ((qf_ref, kf_ref, vf_ref, gf_ref, bf_ref, of_ref, cf_ref),
            (qb_ref, kb_ref, vb_ref, gb_ref, bb_ref, ob_ref, cb_ref))
    chains, dests = [], []
    for hh in range(gf_ref.shape[0]):
        cs = slice(hh * GDN_D, (hh + 1) * GDN_D)
        for d, (q_ref, k_ref, v_ref, g_ref, b_ref, o_ref, c_ref) in enumerate(dirs):
            chains.append((q_ref[:, cs].astype(F32), k_ref[:, cs].astype(F32), v_ref[:, cs].astype(F32),
                           g_ref[hh], b_ref[hh], s_ref[2 * hh + d], d == 0))
            dests.append((o_ref, c_ref, cs, 2 * hh + d))
    results = _gdn_chains(chains)
    for (o_ref, _, cs, idx), (o, s_new) in zip(dests, results):
        s_ref[idx] = s_new
        o_ref[:, cs] = o

    @pl.when(t == 0)
    def _():
        for (_, c_ref, cs, _), (o, _) in zip(dests, results):
            c_ref[:, cs] = o


def gdn_scan(qkv, gates, n_ctx, heads_per_step=4):
    bsz, rows, _ = qkv.shape
    assert n_ctx == ROW_TILE
    h, hb = GDN_HEADS, heads_per_step
    nt = rows // ROW_TILE
    seq = rows - n_ctx
    wid = hb * GDN_D
    tile_f = lambda t: t
    tile_b = lambda t: jnp.where(t == 0, 0, nt - t)
    out_f = lambda t: jnp.maximum(t - 1, 0)
    out_b = lambda t: jnp.where(t == 0, nt - 2, nt - 1 - t)
    col = lambda part, tile: pl.BlockSpec((None, ROW_TILE, wid), lambda b, hg, t: (b, tile(t), part * (h // hb) + hg))
    gate = lambda kind, tile: pl.BlockSpec((None, hb, ROW_TILE, LANES),
                                           lambda b, hg, t: (b, kind * (h // hb) + hg, tile(t), 0))
    out = lambda tile: pl.BlockSpec((None, ROW_TILE, wid), lambda b, hg, t: (b, tile(t), hg))
    shp = jax.ShapeDtypeStruct((bsz, seq, h * GDN_D), F32)
    shp_c = jax.ShapeDtypeStruct((bsz, n_ctx, h * GDN_D), F32)
    ctx_tile = lambda t: 0
    return pl.pallas_call(
        _gdn_kernel,
        grid=(bsz, h // hb, nt),
        in_specs=[col(0, tile_f), col(1, tile_f), col(2, tile_f), col(0, tile_b), col(1, tile_b), col(2, tile_b),
                  gate(0, tile_f), gate(2, tile_f), gate(1, tile_b), gate(3, tile_b)],
        out_specs=[out(out_f), out(out_b), out(ctx_tile), out(ctx_tile)],
        out_shape=[shp, shp, shp_c, shp_c],
        scratch_shapes=[pltpu.VMEM((2 * hb, GDN_D, GDN_D), F32)],
        compiler_params=_params("parallel", "parallel", "arbitrary"),
        name="gdn_scan",
    )(qkv, qkv, qkv, qkv, qkv, qkv, gates, gates, gates, gates)


def _gdn_out_kernel(of_ref, ob_ref, z_ref, gn_ref, y_ref):
    gn = gn_ref[...]
    for hh in range(GDN_HEADS):
        cs = slice(hh * GDN_D, (hh + 1) * GDN_D)
        o = of_ref[:, cs] + ob_ref[:, cs]
        y_ref[:, cs] = (_rms(o, 1e-6) * gn * _silu(z_ref[:, cs].astype(F32))).astype(y_ref.dtype)


def gdn_out(o_f, o_b, p, z_block, row_off_blocks, gn):
    bsz, seq, width = o_f.shape
    row = pl.BlockSpec((None, ROW_TILE, width), lambda b, t: (b, t, 0))
    return pl.pallas_call(
        _gdn_out_kernel,
        grid=(bsz, seq // ROW_TILE),
        in_specs=[row, row,
                  pl.BlockSpec((None, ROW_TILE, width), lambda b, t: (b, t + row_off_blocks, z_block)),
                  pl.BlockSpec((1, GDN_D), lambda b, t: (0, 0))],
        out_specs=row,
        out_shape=jax.ShapeDtypeStruct((bsz, seq, width), BF16),
        compiler_params=_params("parallel", "parallel"),
        name="gdn_out",
    )(o_f, o_b, p, gn.reshape(1, GDN_D))


def _rope_kernel(q_ref, k_ref, cos_ref, sa_ref, sb_ref, qo_ref, ko_ref, qc_ref, *, n_ctx_tiles):
    cos, sa, sb = cos_ref[...], sa_ref[...], sb_ref[...]

    def rope(x):
        return x * cos + pltpu.roll(x, LANES - DIFF_D // 4, 1) * sa + pltpu.roll(x, DIFF_D // 4, 1) * sb

    q_scale = (DIFF_D ** -0.5) * math.log2(math.e)
    for j in range(q_ref.shape[1] // DIFF_D):
        cs = slice(j * DIFF_D, (j + 1) * DIFF_D)
        ko_ref[:, cs] = rope(k_ref[:, cs].astype(F32)).astype(ko_ref.dtype)

    def write_q(dst_ref):
        for j in range(q_ref.shape[1] // DIFF_D):
            cs = slice(j * DIFF_D, (j + 1) * DIFF_D)
            dst_ref[:, cs] = rope(q_ref[:, cs].astype(F32) * q_scale).astype(dst_ref.dtype)

    pl.when(pl.program_id(1) >= n_ctx_tiles)(lambda: write_q(qo_ref))
    pl.when(pl.program_id(1) < n_ctx_tiles)(lambda: write_q(qc_ref))


def rope_tables(n_ctx, seq):
    half = DIFF_D // 2
    rows = seq // GRID_W
    inv = 1.0 / (ROPE_BASE ** (jnp.arange(0, half, 2, dtype=F32) / half))
    r = jnp.repeat(jnp.arange(rows), GRID_W).astype(F32)[:, None]
    col = jnp.tile(jnp.arange(GRID_W), rows).astype(F32)[:, None]
    ang = jnp.concatenate([r * inv, r * inv, col * inv, col * inv], axis=-1)
    cos, sin = jnp.cos(ang), jnp.sin(ang)
    first_quarter = (jnp.arange(DIFF_D) % half) < half // 2
    sin_a = jnp.where(first_quarter, -sin, 0.0)
    sin_b = jnp.where(first_quarter, 0.0, sin)
    ones = jnp.ones((n_ctx, DIFF_D), F32)
    zeros = jnp.zeros((n_ctx, DIFF_D), F32)
    cat = lambda a, b: jnp.concatenate([a, b], axis=0)
    return cat(ones, cos), cat(zeros, sin_a), cat(zeros, sin_b)


def rope_qk(p, n_ctx, seq):
    bsz, rows, _ = p.shape
    width = DIFF_HEADS * 2 * DIFF_D
    n_ctx_tiles = n_ctx // ROW_TILE
    cos, sa, sb = rope_tables(n_ctx, seq)
    blk = lambda off: pl.BlockSpec((None, ROW_TILE, width), lambda b, t: (b, t, off))
    tab = pl.BlockSpec((ROW_TILE, DIFF_D), lambda b, t: (t, 0))
    return pl.pallas_call(
        functools.partial(_rope_kernel, n_ctx_tiles=n_ctx_tiles),
        grid=(bsz, rows // ROW_TILE),
        in_specs=[blk(0), blk(1), tab, tab, tab],
        out_specs=[pl.BlockSpec((None, ROW_TILE, width), lambda b, t: (b, jnp.maximum(t - n_ctx_tiles, 0), 0)),
                   pl.BlockSpec((None, ROW_TILE, width), lambda b, t: (b, t, 0)),
                   pl.BlockSpec((None, ROW_TILE, width), lambda b, t: (b, jnp.minimum(t, n_ctx_tiles - 1), 0))],
        out_shape=[jax.ShapeDtypeStruct((bsz, seq, width), BF16), jax.ShapeDtypeStruct((bsz, rows, width), BF16),
                   jax.ShapeDtypeStruct((bsz, n_ctx, width), BF16)],
        compiler_params=_params("parallel", "arbitrary"),
        name="rope_qk",
    )(p, p, cos, sa, sb)


def _diff_attn_kernel(q_ref, k_ref, v_ref, lam_ref, dn_ref, o_ref, *, lambda_init):
    lv = lam_ref[...]
    lam = (jnp.exp(jnp.sum(lv[0:1] * lv[1:2], axis=-1, keepdims=True))
           - jnp.exp(jnp.sum(lv[2:3] * lv[3:4], axis=-1, keepdims=True)) + lambda_init)

    sub = q_ref.shape[0] // ATTN_SUBTILES
    tiles = [slice(i * sub, (i + 1) * sub) for i in range(ATTN_SUBTILES)]
    maps = [slice(m * DIFF_D, (m + 1) * DIFF_D) for m in range(2)]
    scores = [[_dot_nt(q_ref[rs, cs], k_ref[:, cs]) for cs in maps] for rs in tiles]
    weights = []
    for s0, s1 in scores:
        p0 = jnp.exp2(s0 - jnp.max(s0, axis=-1, keepdims=True))
        p1 = jnp.exp2(s1 - jnp.max(s1, axis=-1, keepdims=True))
        r0 = 1.0 / jnp.sum(p0, axis=-1, keepdims=True)
        r1 = 1.0 / jnp.sum(p1, axis=-1, keepdims=True)
        weights.append((p0 * r0 - p1 * (lam * r1)).astype(BF16))
    outs = [_dot(a, v_ref[...]) for a in weights]
    for rs, o in zip(tiles, outs):
        y = _rms(o, 1e-5) * dn_ref[...] * (1.0 - lambda_init)
        o_ref[rs, :] = y.astype(o_ref.dtype)


def diff_attention(q, k, v, v_block, lam_vecs, diff_norm, lambda_init, tq=512, kv_rows=None):
    bsz, seq, width = q.shape
    rows = k.shape[1] if kv_rows is None else kv_rows
    tq = min(tq, seq)
    hw = 2 * DIFF_D
    kern = functools.partial(_diff_attn_kernel, lambda_init=lambda_init)
    kv = pl.BlockSpec((None, rows, hw), lambda b, h, i: (b, 0, h))
    return pl.pallas_call(
        kern,
        grid=(bsz, DIFF_HEADS, seq // tq),
        in_specs=[pl.BlockSpec((None, tq, hw), lambda b, h, i: (b, i, h)), kv,
                  pl.BlockSpec((None, rows, hw), lambda b, h, i: (b, 0, v_block + h)),
                  pl.BlockSpec((4, DIFF_D), lambda b, h, i: (0, 0)),
                  pl.BlockSpec((1, hw), lambda b, h, i: (0, 0))],
        out_specs=pl.BlockSpec((None, tq, hw), lambda b, h, i: (b, i, h)),
        out_shape=jax.ShapeDtypeStruct((bsz, seq, width), BF16),
        compiler_params=_params("parallel", "parallel", "parallel"),
        name="diff_attention",
    )(q, k, v, lam_vecs, diff_norm.reshape(1, hw))


def _conf_conv_kernel(a_ref, b_ref, w_ref, bias_ref, o_ref, buf_ref):
    rows = a_ref.shape[0]
    zeros = jnp.zeros((CONF_PAD, LANES), F32)
    buf_ref[0:CONF_PAD, :] = zeros
    buf_ref[CONF_PAD + rows:2 * CONF_PAD + rows, :] = zeros
    buf_ref[CONF_PAD:CONF_PAD + rows, :] = a_ref[...].astype(F32) * _sigmoid(b_ref[...].astype(F32))
    w = w_ref[...]
    bias = bias_ref[...]
    first = CONF_PAD - (CONF_K - 1) // 2

    def tile(t, carry):
        base = pl.multiple_of(t * ROW_TILE, ROW_TILE)
        acc = jnp.broadcast_to(bias, (ROW_TILE, LANES))
        for kk in range(CONF_K):
            acc = acc + w[kk:kk + 1, :] * buf_ref[pl.ds(base + first + kk, ROW_TILE), :]
        o_ref[pl.ds(base, ROW_TILE), :] = acc
        return carry

    lax.fori_loop(0, rows // ROW_TILE, tile, 0)


def conf_conv(p, dw, dw_b):
    bsz, rows, _ = p.shape
    ch = dw.shape[1]
    nb = ch // LANES
    return pl.pallas_call(
        _conf_conv_kernel,
        grid=(bsz, nb),
        in_specs=[pl.BlockSpec((None, rows, LANES), lambda b, j: (b, 0, j)),
                  pl.BlockSpec((None, rows, LANES), lambda b, j: (b, 0, nb + j)),
                  pl.BlockSpec((CONF_K, LANES), lambda b, j: (0, j)),
                  pl.BlockSpec((1, LANES), lambda b, j: (0, j))],
        out_specs=pl.BlockSpec((None, rows, LANES), lambda b, j: (b, 0, j)),
        out_shape=jax.ShapeDtypeStruct((bsz, rows, ch), F32),
        scratch_shapes=[pltpu.VMEM((rows + 2 * CONF_PAD, LANES), F32)],
        compiler_params=_params("parallel", "parallel"),
        name="conf_conv",
    )(p, p, dw, dw_b.reshape(1, ch))


def _ln_silu_kernel(x_ref, g_ref, b_ref, o_ref):
    x = x_ref[...]
    xc = x - jnp.mean(x, axis=-1, keepdims=True)
    y = xc * lax.rsqrt(jnp.mean(xc * xc, axis=-1, keepdims=True) + 1e-5) * g_ref[...] + b_ref[...]
    o_ref[...] = _silu(y).astype(o_ref.dtype)


def ln_silu(x, g, b):
    bsz, rows, ch = x.shape
    row = pl.BlockSpec((None, ROW_TILE, ch), lambda bb, t: (bb, t, 0))
    vec = pl.BlockSpec((1, ch), lambda bb, t: (0, 0))
    return pl.pallas_call(
        _ln_silu_kernel,
        grid=(bsz, rows // ROW_TILE),
        in_specs=[row, vec, vec],
        out_specs=row,
        out_shape=jax.ShapeDtypeStruct((bsz, rows, ch), BF16),
        compiler_params=_params("parallel", "parallel"),
        name="ln_silu",
    )(x, g.reshape(1, ch), b.reshape(1, ch))


def _short_conv_kernel(gb_ref, gc_ref, sh_ref, w_ref, o_ref):
    m = gc_ref[...].astype(F32) * sh_ref[...].astype(F32)
    rows = m.shape[0]
    row = lax.broadcasted_iota(jnp.int32, m.shape, 0)
    prev = jnp.where(row == 0, 0.0, pltpu.roll(m, 1, 0))
    nxt = jnp.where(row == rows - 1, 0.0, pltpu.roll(m, rows - 1, 0))
    w = w_ref[...]
    o_ref[...] = (gb_ref[...].astype(F32) * (prev * w[0:1] + m * w[1:2] + nxt * w[2:3])).astype(o_ref.dtype)


def short_conv(p, first_block, w):
    bsz, rows, _ = p.shape
    ch = w.shape[1]
    nb = ch // LANES
    blk = lambda off: pl.BlockSpec((None, rows, LANES), lambda b, j: (b, 0, first_block + off + j))
    return pl.pallas_call(
        _short_conv_kernel,
        grid=(bsz, nb),
        in_specs=[blk(0), blk(nb), blk(2 * nb), pl.BlockSpec((w.shape[0], LANES), lambda b, j: (0, j))],
        out_specs=pl.BlockSpec((None, rows, LANES), lambda b, j: (b, 0, j)),
        out_shape=jax.ShapeDtypeStruct((bsz, rows, ch), BF16),
        compiler_params=_params("parallel", "parallel"),
        name="short_conv",
    )(p, p, p, w)


def _ffn(xn, w_gate, w_up, w_down, layer):
    pad = -w_gate.shape[2] % FF_ALIGN
    wg = cast_weight(w_gate, layer, pad_cols=pad)
    wu = cast_weight(w_up, layer, pad_cols=pad)
    wd = cast_weight(w_down, layer, pad_rows=pad)
    act = ff_up(xn, wg, wu)
    return matmul_ksplit(act, wd, k_steps=8)


def _mixer_even(xn_all, n_ctx, seq, w_in_all, w_out_all, conv_w, a_log, dt_bias, gdn_norm, lam_vecs, diff_norm,
                lambda_init):
    bsz, rows, d = xn_all.shape
    qk_w = GDN_HEADS * GDN_D
    c_ab = 3 * qk_w + qk_w
    n_ab = 4 * GDN_HEADS
    w_in = w_in_all[0].astype(BF16)
    w_ab = jnp.pad(w_in[:, c_ab:c_ab + n_ab], ((0, 0), (0, LANES - n_ab)))
    x2 = xn_all.reshape(bsz * rows, d)
    pa = matmul(x2, w_in, BF16, n_cols=c_ab).reshape(bsz, rows, -1)
    pb = matmul(x2, w_in[:, c_ab + n_ab:], BF16).reshape(bsz, rows, -1)
    ab = matmul(x2, w_ab, F32, tn=LANES).reshape(bsz, rows, LANES)
    qkv = gdn_prep(pa, conv_w, n_ctx)
    gates = gdn_gates(ab, a_log, dt_bias)
    o_f, o_b, c_f, c_b = gdn_scan(qkv, gates, n_ctx)
    z_block = c_ab // (DIFF_HEADS * 2 * DIFF_D) - 1
    ya = gdn_out(o_f, o_b, pa, z_block, n_ctx // ROW_TILE, gdn_norm)
    qr, kr, qc = rope_qk(pb, n_ctx, seq)
    yb = diff_attention(qr, kr, pb, 2 * DIFF_HEADS, lam_vecs, diff_norm, lambda_init)
    w_out = cast_weight(w_out_all, 0)
    y = matmul_cat(ya.reshape(bsz * seq, -1), yb.reshape(bsz * seq, -1), w_out)
    ya_c = gdn_out(c_f, c_b, pa, z_block, 0, gdn_norm)
    yb_c = diff_attention(qc, kr, pb, 2 * DIFF_HEADS, lam_vecs, diff_norm, lambda_init, kv_rows=n_ctx)
    y_c = matmul_cat(ya_c.reshape(bsz * n_ctx, -1), yb_c.reshape(bsz * n_ctx, -1), w_out)
    return y.reshape(bsz, seq, d), y_c.reshape(bsz, n_ctx, d)


def _mixer_odd(xn, w_in_all, w_out_all, conf_dw, conf_dw_b, conf_ln_g, conf_ln_b, sc_w):
    bsz, seq, d = xn.shape
    p = matmul(xn.reshape(bsz * seq, d), cast_weight(w_in_all, 0), BF16).reshape(bsz, seq, -1)
    yc = ln_silu(conf_conv(p, conf_dw, conf_dw_b), conf_ln_g, conf_ln_b)
    yd = short_conv(p, 2 * (conf_dw.shape[1] // LANES), sc_w)
    y = matmul_cat(yc.reshape(bsz * seq, -1), yd.reshape(bsz * seq, -1), cast_weight(w_out_all, 0))
    return y.reshape(bsz, seq, d)


@jax.jit
def _forward(x, c, ctx, c_ctx, w_ada, b_ada, g_mix_pre, g_mix_post, g_ffn_pre, g_ffn_post,
             w_ff_gate, w_ff_up, w_ff_down, w_in_even, w_out_even, gdn_conv, gdn_a_log,
             gdn_dt_bias, gdn_norm, diff_lambda, diff_norm, w_in_odd, w_out_odd, conf_dw,
             conf_dw_b, conf_ln_g, conf_ln_b, sc_conv):
    bsz, seq, d = x.shape
    n_ctx = ctx.shape[1]
    depth = w_ada.shape[0]
    assert depth == 2, "the context stream is only carried through the first (even) layer"
    cond = jnp.concatenate([c, c_ctx[None, :], jnp.zeros((SUBLANES - bsz - 1, d), F32)], axis=0)
    mods = [adaln(cond, w_ada, b_ada, i).reshape(SUBLANES, 6, 1, d) for i in range(depth)]

    xn_all = prenorm_with_ctx(x, ctx, g_mix_pre[0], mods[0], bsz)
    y, y_c = _mixer_even(xn_all, n_ctx, seq, w_in_even, w_out_even, gdn_conv[0], gdn_a_log[0],
                         gdn_dt_bias[0], gdn_norm[0], diff_lambda[0], diff_norm[0], 0.8 - 0.6 * math.exp(-0.3 * 0))
    x, xn = postnorm(x, y, mods[0], 2, g_mix_post[0], (g_ffn_pre[0], mods[0], 3, 4))
    f = _ffn(xn.reshape(bsz * seq, d), w_ff_gate, w_ff_up, w_ff_down, 0).reshape(bsz, seq, d)
    x, xn = postnorm(x, f, mods[0], 5, g_ffn_post[0], (g_mix_pre[1], mods[1], 0, 1))
    h, hn = postnorm(ctx, y_c, mods[0], 2, g_mix_post[0], (g_ffn_pre[0], mods[0], 3, 4), mod_row=bsz)
    f_c = _ffn(hn.reshape(bsz * n_ctx, d), w_ff_gate, w_ff_up, w_ff_down, 0).reshape(bsz, n_ctx, d)
    h = postnorm(h, f_c, mods[0], 5, g_ffn_post[0], mod_row=bsz)
    del h

    y = _mixer_odd(xn, w_in_odd, w_out_odd, conf_dw[0], conf_dw_b[0], conf_ln_g[0], conf_ln_b[0],
                   sc_conv[0])
    x, xn = postnorm(x, y, mods[1], 2, g_mix_post[1], (g_ffn_pre[1], mods[1], 3, 4))
    f = _ffn(xn.reshape(bsz * seq, d), w_ff_gate, w_ff_up, w_ff_down, 1).reshape(bsz, seq, d)
    return postnorm(x, f, mods[1], 5, g_ffn_post[1])


def kernel(x, c, ctx, c_ctx, w_ada, b_ada, g_mix_pre, g_mix_post, g_ffn_pre, g_ffn_post, w_ff_gate, w_ff_up,
           w_ff_down, w_in_even, w_out_even, gdn_conv, gdn_a_log, gdn_dt_bias, gdn_norm, diff_lambda, diff_norm,
           w_in_odd, w_out_odd, conf_dw, conf_dw_b, conf_ln_g, conf_ln_b, sc_conv):
    return _forward(x, c, ctx, c_ctx, w_ada, b_ada, g_mix_pre, g_mix_post, g_ffn_pre, g_ffn_post, w_ff_gate,
                    w_ff_up, w_ff_down, w_in_even, w_out_even, gdn_conv, gdn_a_log, gdn_dt_bias, gdn_norm,
                    diff_lambda, diff_norm, w_in_odd, w_out_odd, conf_dw, conf_dw_b, conf_ln_g, conf_ln_b,
                    sc_conv)
```

```python
import functools
import math

import jax
import jax.numpy as jnp
from jax import lax
from jax.experimental import pallas as pl
from jax.experimental.pallas import tpu as pltpu

F32 = jnp.float32
BF16 = jnp.bfloat16

GRID_W = 64
GDN_HEADS = 16
GDN_D = 128
GDN_CHUNK = 64
DIFF_HEADS = 8
DIFF_D = 128
ROPE_BASE = 10000.0
CONF_K = 31
CONF_PAD = 16

LANES = 128
SUBLANES = 8
VMEM_LIMIT = 56 * 1024 * 1024
ROW_TILE = 256
MM_TILE = 1024
FF_TILE = 512
FF_ALIGN = 1024
ATTN_SUBTILES = 2


def _params(*sem):
    return pltpu.CompilerParams(dimension_semantics=sem, vmem_limit_bytes=VMEM_LIMIT)


def _sigmoid(x):
    return 1.0 / (1.0 + jnp.exp(-x))


def _silu(x):
    return x * _sigmoid(x)


def _dot(a, b):
    return jnp.dot(a, b, preferred_element_type=F32)


def _dot_nt(a, b):
    return lax.dot_general(a, b, (((1,), (1,)), ((), ())), preferred_element_type=F32)


def _dot_tn(a, b):
    return lax.dot_general(a, b, (((0,), (0,)), ((), ())), preferred_element_type=F32)


def _hi_lo(x):
    hi = x.astype(BF16)
    return hi, (x - hi.astype(F32)).astype(BF16)


def _split_lhs(a):
    hi, lo = _hi_lo(a)
    return jnp.concatenate([hi, hi, lo], axis=1)


def _split_rhs(b):
    hi, lo = _hi_lo(b)
    return jnp.concatenate([hi, lo, hi], axis=0)


def _adaln_kernel(c_ref, w_ref, b_ref, o_ref):
    s = _silu(c_ref[...]).astype(BF16)
    o_ref[...] = _dot(s, w_ref[...].astype(BF16)) + b_ref[...]


def adaln(cond, w, b, layer):
    rows, d = cond.shape
    depth, _, n = w.shape
    tn = 512
    return pl.pallas_call(
        _adaln_kernel,
        grid=(n // tn,),
        in_specs=[pl.BlockSpec((rows, d), lambda j: (0, 0)),
                  pl.BlockSpec((None, d, tn), lambda j: (layer, 0, j)),
                  pl.BlockSpec((None, 1, tn), lambda j: (layer, 0, j))],
        out_specs=pl.BlockSpec((rows, tn), lambda j: (0, j)),
        out_shape=jax.ShapeDtypeStruct((rows, n), F32),
        compiler_params=_params("parallel"),
        name="adaln",
    )(cond, w, b.reshape(depth, 1, n))


def _rms(x, eps):
    return x * lax.rsqrt(jnp.mean(x * x, axis=-1, keepdims=True) + eps)


def _prenorm_ctx_kernel(x_ref, c_ref, g_ref, sh_ref, sc_ref, o_ref):
    xv = jnp.where(pl.program_id(1) == 0, c_ref[...], x_ref[...])
    y = _rms(xv, 1e-6) * g_ref[...]
    o_ref[...] = (y * (1.0 + sc_ref[...]) + sh_ref[...]).astype(o_ref.dtype)


def prenorm_with_ctx(x, ctx, g, mods, ctx_row):
    bsz, seq, d = x.shape
    n_ctx = ctx.shape[1]
    assert n_ctx == ROW_TILE
    nt = (seq + n_ctx) // ROW_TILE
    mrow = lambda b, t: jnp.where(t == 0, ctx_row, b)
    return pl.pallas_call(
        _prenorm_ctx_kernel,
        grid=(bsz, nt),
        in_specs=[pl.BlockSpec((None, ROW_TILE, d), lambda b, t: (b, jnp.maximum(t - 1, 0), 0)),
                  pl.BlockSpec((None, ROW_TILE, d), lambda b, t: (b, 0, 0)),
                  pl.BlockSpec((1, d), lambda b, t: (0, 0)),
                  pl.BlockSpec((None, None, 1, d), lambda b, t: (mrow(b, t), 0, 0, 0)),
                  pl.BlockSpec((None, None, 1, d), lambda b, t: (mrow(b, t), 1, 0, 0))],
        out_specs=pl.BlockSpec((None, ROW_TILE, d), lambda b, t: (b, t, 0)),
        out_shape=jax.ShapeDtypeStruct((bsz, seq + n_ctx, d), BF16),
        compiler_params=_params("parallel", "parallel"),
        name="prenorm_ctx",
    )(x, ctx, g.reshape(1, d), mods, mods)


def _postnorm_kernel(x_ref, y_ref, gate_ref, gpost_ref, gpre_ref, sh_ref, sc_ref, xo_ref, no_ref):
    xn = x_ref[...] + gate_ref[...] * (_rms(y_ref[...].astype(F32), 1e-6) * gpost_ref[...])
    xo_ref[...] = xn
    h = _rms(xn, 1e-6) * gpre_ref[...]
    no_ref[...] = (h * (1.0 + sc_ref[...]) + sh_ref[...]).astype(no_ref.dtype)


def _postnorm_last_kernel(x_ref, y_ref, gate_ref, gpost_ref, xo_ref):
    xo_ref[...] = x_ref[...] + gate_ref[...] * (_rms(y_ref[...].astype(F32), 1e-6) * gpost_ref[...])


def postnorm(x, y, mods, gate_idx, g_post, nxt=None, mod_row=None):
    bsz, seq, d = x.shape
    row = pl.BlockSpec((None, ROW_TILE, d), lambda b, t: (b, t, 0))
    vec = pl.BlockSpec((1, d), lambda b, t: (0, 0))
    mod = lambda k: pl.BlockSpec((None, None, 1, d), lambda b, t: (b if mod_row is None else mod_row, k, 0, 0))
    grid = (bsz, seq // ROW_TILE)
    if nxt is None:
        return pl.pallas_call(
            _postnorm_last_kernel, grid=grid,
            in_specs=[row, row, mod(gate_idx), vec],
            out_specs=row,
            out_shape=jax.ShapeDtypeStruct((bsz, seq, d), F32),
            compiler_params=_params("parallel", "parallel"),
            name="postnorm_last",
        )(x, y, mods, g_post.reshape(1, d))
    g_pre, mods_n, shift_idx, scale_idx = nxt
    return pl.pallas_call(
        _postnorm_kernel, grid=grid,
        in_specs=[row, row, mod(gate_idx), vec, vec, mod(shift_idx), mod(scale_idx)],
        out_specs=[row, row],
        out_shape=[jax.ShapeDtypeStruct((bsz, seq, d), F32), jax.ShapeDtypeStruct((bsz, seq, d), BF16)],
        compiler_params=_params("parallel", "parallel"),
        name="postnorm",
    )(x, y, mods, g_post.reshape(1, d), g_pre.reshape(1, d), mods_n, mods_n)


def _cast_pad_kernel(w_ref, o_ref, *, n_row_tiles):
    cols = w_ref.shape[1]

    @pl.when(pl.program_id(0) < n_row_tiles)
    def _():
        o_ref[:, :cols] = w_ref[...].astype(o_ref.dtype)
        if o_ref.shape[1] > cols:
            o_ref[:, cols:] = jnp.zeros((o_ref.shape[0], o_ref.shape[1] - cols), o_ref.dtype)

    @pl.when(pl.program_id(0) >= n_row_tiles)
    def _():
        o_ref[...] = jnp.zeros(o_ref.shape, o_ref.dtype)


def cast_weight(w, layer, n_cols=None, pad_rows=0, pad_cols=0):
    _, rows, cols = w.shape
    cols = cols if n_cols is None else n_cols
    assert rows % ROW_TILE == 0 and pad_rows % ROW_TILE == 0
    assert pad_cols == 0 or (cols % LANES == 0 and pad_cols % LANES == 0)
    n_row_tiles = rows // ROW_TILE
    kern = functools.partial(_cast_pad_kernel, n_row_tiles=n_row_tiles)
    return pl.pallas_call(
        kern,
        grid=((rows + pad_rows) // ROW_TILE,),
        in_specs=[pl.BlockSpec((None, ROW_TILE, cols), lambda i: (layer, jnp.minimum(i, n_row_tiles - 1), 0))],
        out_specs=pl.BlockSpec((ROW_TILE, cols + pad_cols), lambda i: (i, 0)),
        out_shape=jax.ShapeDtypeStruct((rows + pad_rows, cols + pad_cols), BF16),
        compiler_params=_params("parallel"),
        name="cast_weight",
    )(w)


def _mm_kernel(x_ref, w_ref, o_ref):
    o_ref[...] = _dot(x_ref[...], w_ref[...]).astype(o_ref.dtype)


def matmul(x, w, out_dtype, tn=MM_TILE, n_cols=None):
    m, k = x.shape
    n = w.shape[1] if n_cols is None else n_cols
    tm = MM_TILE
    return pl.pallas_call(
        _mm_kernel,
        grid=(m // tm, n // tn),
        in_specs=[pl.BlockSpec((tm, k), lambda i, j: (i, 0)),
                  pl.BlockSpec((k, tn), lambda i, j: (0, j))],
        out_specs=pl.BlockSpec((tm, tn), lambda i, j: (i, j)),
        out_shape=jax.ShapeDtypeStruct((m, n), out_dtype),
        compiler_params=_params("parallel", "parallel"),
        name="matmul",
    )(x, w)


def _mm2_kernel(a_ref, b_ref, w_ref, o_ref):
    ka = a_ref.shape[1]
    o_ref[...] = (_dot(a_ref[...], w_ref[:ka, :]) + _dot(b_ref[...], w_ref[ka:, :])).astype(o_ref.dtype)


def matmul_cat(a, b, w):
    m, ka = a.shape
    kb = b.shape[1]
    n = w.shape[1]
    tm = tn = MM_TILE
    return pl.pallas_call(
        _mm2_kernel,
        grid=(m // tm, n // tn),
        in_specs=[pl.BlockSpec((tm, ka), lambda i, j: (i, 0)),
                  pl.BlockSpec((tm, kb), lambda i, j: (i, 0)),
                  pl.BlockSpec((ka + kb, tn), lambda i, j: (0, j))],
        out_specs=pl.BlockSpec((tm, tn), lambda i, j: (i, j)),
        out_shape=jax.ShapeDtypeStruct((m, n), BF16),
        compiler_params=_params("parallel", "parallel"),
        name="matmul_cat",
    )(a, b, w)


def _ff_up_kernel(x_ref, wg_ref, wu_ref, o_ref):
    x = x_ref[...]
    o_ref[...] = (_silu(_dot(x, wg_ref[...])) * _dot(x, wu_ref[...])).astype(o_ref.dtype)


def ff_up(x, wg, wu):
    m, k = x.shape
    n = wg.shape[1]
    tm, tn = MM_TILE, FF_TILE
    wspec = pl.BlockSpec((k, tn), lambda i, j: (0, j))
    return pl.pallas_call(
        _ff_up_kernel,
        grid=(m // tm, n // tn),
        in_specs=[pl.BlockSpec((tm, k), lambda i, j: (i, 0)), wspec, wspec],
        out_specs=pl.BlockSpec((tm, tn), lambda i, j: (i, j)),
        out_shape=jax.ShapeDtypeStruct((m, n), BF16),
        compiler_params=_params("parallel", "parallel"),
        name="ff_up",
    )(x, wg, wu)


def _mm_acc_kernel(x_ref, w_ref, o_ref, acc_ref):
    part = _dot(x_ref[...], w_ref[...])
    kk = pl.program_id(2)
    last = pl.num_programs(2) - 1

    @pl.when(kk == 0)
    def _():
        acc_ref[...] = part

    @pl.when((kk > 0) & (kk < last))
    def _():
        acc_ref[...] += part

    @pl.when(kk == last)
    def _():
        o_ref[...] = (acc_ref[...] + part).astype(o_ref.dtype)


def matmul_ksplit(x, w, k_steps, tm=MM_TILE):
    m, k = x.shape
    n = w.shape[1]
    tm = min(tm, m)
    tn = MM_TILE
    tk = k // k_steps
    assert k_steps >= 2 and tk * k_steps == k and tk % LANES == 0
    return pl.pallas_call(
        _mm_acc_kernel,
        grid=(m // tm, n // tn, k_steps),
        in_specs=[pl.BlockSpec((tm, tk), lambda i, j, kk: (i, kk)),
                  pl.BlockSpec((tk, tn), lambda i, j, kk: (kk, j))],
        out_specs=pl.BlockSpec((tm, tn), lambda i, j, kk: (i, j)),
        out_shape=jax.ShapeDtypeStruct((m, n), BF16),
        scratch_shapes=[pltpu.VMEM((tm, tn), F32)],
        compiler_params=_params("parallel", "parallel", "arbitrary"),
        name="matmul_ksplit",
    )(x, w)


def _gdn_prep_kernel(p_ref, w_ref, o_ref, *, n_ctx, n_norm_blocks):
    x = p_ref[...].astype(F32)
    rows = x.shape[0]
    row = lax.broadcasted_iota(jnp.int32, x.shape, 0)
    prev = jnp.where((row == 0) | (row == n_ctx), 0.0, pltpu.roll(x, 1, 0))
    nxt = jnp.where((row == n_ctx - 1) | (row == rows - 1), 0.0, pltpu.roll(x, rows - 1, 0))
    w = w_ref[...]
    y = _silu(prev * w[0:1] + x * w[1:2] + nxt * w[2:3])
    yn = y * lax.rsqrt(jnp.sum(y * y, axis=-1, keepdims=True) + 1e-6)
    o_ref[...] = jnp.where(pl.program_id(1) < n_norm_blocks, yn, y).astype(o_ref.dtype)


def gdn_prep(p, conv_w, n_ctx):
    bsz, rows, _ = p.shape
    n_blocks = conv_w.shape[1] // LANES
    kern = functools.partial(_gdn_prep_kernel, n_ctx=n_ctx, n_norm_blocks=2 * GDN_HEADS)
    return pl.pallas_call(
        kern,
        grid=(bsz, n_blocks),
        in_specs=[pl.BlockSpec((None, rows, LANES), lambda b, j: (b, 0, j)),
                  pl.BlockSpec((conv_w.shape[0], LANES), lambda b, j: (0, j))],
        out_specs=pl.BlockSpec((None, rows, LANES), lambda b, j: (b, 0, j)),
        out_shape=jax.ShapeDtypeStruct((bsz, rows, n_blocks * LANES), BF16),
        compiler_params=_params("parallel", "parallel"),
        name="gdn_prep",
    )(p, conv_w)


def _gdn_gates_kernel(ab_ref, alog_ref, dt_ref, o_ref):
    x = ab_ref[...]
    rows = x.shape[0]
    lane = lax.broadcasted_iota(jnp.int32, x.shape, 1)
    r64 = lax.broadcasted_iota(jnp.int32, x.shape, 0) & (GDN_CHUNK - 1)
    z = x + dt_ref[...]
    g = -jnp.exp(alog_ref[...]) * (jnp.maximum(z, 0.0) + jnp.log1p(jnp.exp(-jnp.abs(z))))
    pre = g
    suf = g
    s = 1
    while s < GDN_CHUNK:
        pre = pre + jnp.where(r64 >= s, pltpu.roll(pre, s, 0), 0.0)
        suf = suf + jnp.where(r64 < GDN_CHUNK - s, pltpu.roll(suf, rows - s, 0), 0.0)
        s *= 2
    val = jnp.where(lane < GDN_HEADS, pre, jnp.where(lane < 2 * GDN_HEADS, suf, _sigmoid(x)))
    for j in range(4 * GDN_HEADS):
        o_ref[j] = jnp.broadcast_to(val[:, j:j + 1], (rows, LANES))


def gdn_gates(ab, a_log, dt_bias):
    bsz, rows, _ = ab.shape
    pad = jnp.zeros((LANES - 2 * GDN_HEADS,), F32)
    alog = jnp.concatenate([a_log.reshape(-1), pad]).reshape(1, LANES)
    dt = jnp.concatenate([dt_bias.reshape(-1), pad]).reshape(1, LANES)
    return pl.pallas_call(
        _gdn_gates_kernel,
        grid=(bsz, rows // ROW_TILE),
        in_specs=[pl.BlockSpec((None, ROW_TILE, LANES), lambda b, t: (b, t, 0)),
                  pl.BlockSpec((1, LANES), lambda b, t: (0, 0)),
                  pl.BlockSpec((1, LANES), lambda b, t: (0, 0))],
        out_specs=pl.BlockSpec((None, 4 * GDN_HEADS, ROW_TILE, LANES), lambda b, t: (b, 0, t, 0)),
        out_shape=jax.ShapeDtypeStruct((bsz, 4 * GDN_HEADS, rows, LANES), F32),
        compiler_params=_params("parallel", "parallel"),
        name="gdn_gates",
    )(ab, alog, dt)


def _gdn_chains(chains):
    c = GDN_CHUNK
    n_chunks = chains[0][0].shape[0] // c
    r_iota = lax.broadcasted_iota(jnp.int32, (c, c), 0)
    c_iota = lax.broadcasted_iota(jnp.int32, (c, c), 1)
    eye = (r_iota == c_iota).astype(F32)
    masks = {True: (r_iota >= c_iota, r_iota > c_iota), False: (r_iota <= c_iota, r_iota < c_iota)}

    items = []
    for (q, k, v, g, beta, _, fwd) in chains:
        incl, strict = masks[fwd]
        for n in range(n_chunks):
            rs = slice(n * c, (n + 1) * c)
            qn = q[rs, :] * (GDN_D ** -0.5)
            kn, gn, bn = k[rs, :], g[rs, :], beta[rs, :]
            g_cols = gn.T[:c, :]
            decay = jnp.where(incl, jnp.exp(jnp.where(incl, gn[:, :c] - g_cols, 0.0)), 0.0)
            kb = kn * bn
            eg = jnp.exp(gn)
            g_last = gn[c - 1:c, :] if fwd else gn[0:1, :]
            items.append(dict(
                incl=incl, strict=strict, decay=decay, kbf=kn.astype(BF16),
                lhs=jnp.concatenate([kb, qn], axis=0).astype(BF16),
                rhs=jnp.concatenate([v[rs, :] * bn, kb * eg], axis=1),
                qd=qn * eg, kt=(kn * jnp.exp(g_last - gn)).astype(BF16), e_last=jnp.exp(g_last)))
    for it in items:
        it["both"] = _dot_nt(it["lhs"], it["kbf"])
    for it in items:
        lmat = jnp.where(it["strict"], it["both"][:c, :] * it["decay"], 0.0)
        it["qk"] = jnp.where(it["incl"], it["both"][c:, :] * it["decay"], 0.0).astype(BF16)
        it["tinv"] = eye - lmat
        it["pw"] = lmat
    for it in items:
        it["pw"] = _dot(_split_lhs(it["pw"]), _split_rhs(it["pw"]))
    for stage in range(5):
        for it in items:
            if stage < 4:
                prod = _dot(jnp.concatenate([_split_lhs(it["tinv"]), _split_lhs(it["pw"])], axis=0),
                            _split_rhs(it["pw"]))
                it["tinv"] = it["tinv"] + prod[:c, :]
                it["pw"] = prod[c:, :]
            else:
                it["tinv"] = it["tinv"] + _dot(_split_lhs(it["tinv"]), _split_rhs(it["pw"]))
    for it in items:
        sol = _dot(_split_lhs(it["tinv"]), _split_rhs(it["rhs"]))
        it["u"] = sol[:, :GDN_D]
        it["wq"] = jnp.concatenate([sol[:, GDN_D:], it["qd"]], axis=0).astype(BF16)

    states = [ch[5] for ch in chains]
    outs = [[None] * n_chunks for _ in chains]
    for step in range(n_chunks):
        cur = [items[ci * n_chunks + (step if ch[6] else n_chunks - 1 - step)] for ci, ch in enumerate(chains)]
        ws = [_dot(it["wq"], s.astype(BF16)) for it, s in zip(cur, states)]
        vb = [(it["u"] - w[:c, :]).astype(BF16) for it, w in zip(cur, ws)]
        for ci, (ch, it) in enumerate(zip(chains, cur)):
            n = step if ch[6] else n_chunks - 1 - step
            outs[ci][n] = ws[ci][c:, :] + _dot(it["qk"], vb[ci])
            states[ci] = states[ci] * it["e_last"] + _dot_tn(it["kt"], vb[ci])
    return [(jnp.concatenate(o, axis=0), s) for o, s in zip(outs, states)]


def _gdn_kernel(qf_ref, kf_ref, vf_ref, qb_ref, kb_ref, vb_ref, gf_ref, bf_ref, gb_ref, bb_ref,
                of_ref, ob_ref, cf_ref, cb_ref, s_ref):
    t = pl.program_id(2)

    @pl.when(t == 0)
    def _():
        s_ref[...] = jnp.zeros(s_ref.shape, F32)

    dirs = ((qf_ref, kf_ref, vf_ref, gf_ref, bf_ref, of_ref, cf_ref),
            (qb_ref, kb_ref, vb_ref, gb_ref, bb_ref, ob_ref, cb_ref))
    chains, dests = [], []
    for hh in range(gf_ref.shape[0]):
        cs = slice(hh * GDN_D, (hh + 1) * GDN_D)
        for d, (q_ref, k_ref, v_ref, g_ref, b_ref, o_ref, c_ref) in enumerate(dirs):
            chains.append((q_ref[:, cs].astype(F32), k_ref[:, cs].astype(F32), v_ref[:, cs].astype(F32),
                           g_ref[hh], b_ref[hh], s_ref[2 * hh + d], d == 0))
            dests.append((o_ref, c_ref, cs, 2 * hh + d))
    results = _gdn_chains(chains)
    for (o_ref, _, cs, idx), (o, s_new) in zip(dests, results):
        s_ref[idx] = s_new
        o_ref[:, cs] = o

    @pl.when(t == 0)
    def _():
        for (_, c_ref, cs, _), (o, _) in zip(dests, results):
            c_ref[:, cs] = o


def gdn_scan(qkv, gates, n_ctx, heads_per_step=4):
    bsz, rows, _ = qkv.shape
    assert n_ctx == ROW_TILE
    h, hb = GDN_HEADS, heads_per_step
    nt = rows // ROW_TILE
    seq = rows - n_ctx
    wid = hb * GDN_D
    tile_f = lambda t: t
    tile_b = lambda t: jnp.where(t == 0, 0, nt - t)
    out_f = lambda t: jnp.maximum(t - 1, 0)
    out_b = lambda t: jnp.where(t == 0, nt - 2, nt - 1 - t)
    col = lambda part, tile: pl.BlockSpec((None, ROW_TILE, wid), lambda b, hg, t: (b, tile(t), part * (h // hb) + hg))
    gate = lambda kind, tile: pl.BlockSpec((None, hb, ROW_TILE, LANES),
                                           lambda b, hg, t: (b, kind * (h // hb) + hg, tile(t), 0))
    out = lambda tile: pl.BlockSpec((None, ROW_TILE, wid), lambda b, hg, t: (b, tile(t), hg))
    shp = jax.ShapeDtypeStruct((bsz, seq, h * GDN_D), F32)
    shp_c = jax.ShapeDtypeStruct((bsz, n_ctx, h * GDN_D), F32)
    ctx_tile = lambda t: 0
    return pl.pallas_call(
        _gdn_kernel,
        grid=(bsz, h // hb, nt),
        in_specs=[col(0, tile_f), col(1, tile_f), col(2, tile_f), col(0, tile_b), col(1, tile_b), col(2, tile_b),
                  gate(0, tile_f), gate(2, tile_f), gate(1, tile_b), gate(3, tile_b)],
        out_specs=[out(out_f), out(out_b), out(ctx_tile), out(ctx_tile)],
        out_shape=[shp, shp, shp_c, shp_c],
        scratch_shapes=[pltpu.VMEM((2 * hb, GDN_D, GDN_D), F32)],
        compiler_params=_params("parallel", "parallel", "arbitrary"),
        name="gdn_scan",
    )(qkv, qkv, qkv, qkv, qkv, qkv, gates, gates, gates, gates)


def _gdn_out_kernel(of_ref, ob_ref, z_ref, gn_ref, y_ref):
    gn = gn_ref[...]
    for hh in range(GDN_HEADS):
        cs = slice(hh * GDN_D, (hh + 1) * GDN_D)
        o = of_ref[:, cs] + ob_ref[:, cs]
        y_ref[:, cs] = (_rms(o, 1e-6) * gn * _silu(z_ref[:, cs].astype(F32))).astype(y_ref.dtype)


def gdn_out(o_f, o_b, p, z_block, row_off_blocks, gn):
    bsz, seq, width = o_f.shape
    row = pl.BlockSpec((None, ROW_TILE, width), lambda b, t: (b, t, 0))
    return pl.pallas_call(
        _gdn_out_kernel,
        grid=(bsz, seq // ROW_TILE),
        in_specs=[row, row,
                  pl.BlockSpec((None, ROW_TILE, width), lambda b, t: (b, t + row_off_blocks, z_block)),
                  pl.BlockSpec((1, GDN_D), lambda b, t: (0, 0))],
        out_specs=row,
        out_shape=jax.ShapeDtypeStruct((bsz, seq, width), BF16),
        compiler_params=_params("parallel", "parallel"),
        name="gdn_out",
    )(o_f, o_b, p, gn.reshape(1, GDN_D))


def _rope_kernel(q_ref, k_ref, cos_ref, sa_ref, sb_ref, qo_ref, ko_ref, qc_ref, *, n_ctx_tiles):
    cos, sa, sb = cos_ref[...], sa_ref[...], sb_ref[...]

    def rope(x):
        return x * cos + pltpu.roll(x, LANES - DIFF_D // 4, 1) * sa + pltpu.roll(x, DIFF_D // 4, 1) * sb

    q_scale = (DIFF_D ** -0.5) * math.log2(math.e)
    for j in range(q_ref.shape[1] // DIFF_D):
        cs = slice(j * DIFF_D, (j + 1) * DIFF_D)
        ko_ref[:, cs] = rope(k_ref[:, cs].astype(F32)).astype(ko_ref.dtype)

    def write_q(dst_ref):
        for j in range(q_ref.shape[1] // DIFF_D):
            cs = slice(j * DIFF_D, (j + 1) * DIFF_D)
            dst_ref[:, cs] = rope(q_ref[:, cs].astype(F32) * q_scale).astype(dst_ref.dtype)

    pl.when(pl.program_id(1) >= n_ctx_tiles)(lambda: write_q(qo_ref))
    pl.when(pl.program_id(1) < n_ctx_tiles)(lambda: write_q(qc_ref))


def rope_tables(n_ctx, seq):
    half = DIFF_D // 2
    rows = seq // GRID_W
    inv = 1.0 / (ROPE_BASE ** (jnp.arange(0, half, 2, dtype=F32) / half))
    r = jnp.repeat(jnp.arange(rows), GRID_W).astype(F32)[:, None]
    col = jnp.tile(jnp.arange(GRID_W), rows).astype(F32)[:, None]
    ang = jnp.concatenate([r * inv, r * inv, col * inv, col * inv], axis=-1)
    cos, sin = jnp.cos(ang), jnp.sin(ang)
    first_quarter = (jnp.arange(DIFF_D) % half) < half // 2
    sin_a = jnp.where(first_quarter, -sin, 0.0)
    sin_b = jnp.where(first_quarter, 0.0, sin)
    ones = jnp.ones((n_ctx, DIFF_D), F32)
    zeros = jnp.zeros((n_ctx, DIFF_D), F32)
    cat = lambda a, b: jnp.concatenate([a, b], axis=0)
    return cat(ones, cos), cat(zeros, sin_a), cat(zeros, sin_b)


def rope_qk(p, n_ctx, seq):
    bsz, rows, _ = p.shape
    width = DIFF_HEADS * 2 * DIFF_D
    n_ctx_tiles = n_ctx // ROW_TILE
    cos, sa, sb = rope_tables(n_ctx, seq)
    blk = lambda off: pl.BlockSpec((None, ROW_TILE, width), lambda b, t: (b, t, off))
    tab = pl.BlockSpec((ROW_TILE, DIFF_D), lambda b, t: (t, 0))
    return pl.pallas_call(
        functools.partial(_rope_kernel, n_ctx_tiles=n_ctx_tiles),
        grid=(bsz, rows // ROW_TILE),
        in_specs=[blk(0), blk(1), tab, tab, tab],
        out_specs=[pl.BlockSpec((None, ROW_TILE, width), lambda b, t: (b, jnp.maximum(t - n_ctx_tiles, 0), 0)),
                   pl.BlockSpec((None, ROW_TILE, width), lambda b, t: (b, t, 0)),
                   pl.BlockSpec((None, ROW_TILE, width), lambda b, t: (b, jnp.minimum(t, n_ctx_tiles - 1), 0))],
        out_shape=[jax.ShapeDtypeStruct((bsz, seq, width), BF16), jax.ShapeDtypeStruct((bsz, rows, width), BF16),
                   jax.ShapeDtypeStruct((bsz, n_ctx, width), BF16)],
        compiler_params=_params("parallel", "arbitrary"),
        name="rope_qk",
    )(p, p, cos, sa, sb)


def _diff_attn_kernel(q_ref, k_ref, v_ref, lam_ref, dn_ref, o_ref, *, lambda_init):
    lv = lam_ref[...]
    lam = (jnp.exp(jnp.sum(lv[0:1] * lv[1:2], axis=-1, keepdims=True))
           - jnp.exp(jnp.sum(lv[2:3] * lv[3:4], axis=-1, keepdims=True)) + lambda_init)

    sub = q_ref.shape[0] // ATTN_SUBTILES
    tiles = [slice(i * sub, (i + 1) * sub) for i in range(ATTN_SUBTILES)]
    maps = [slice(m * DIFF_D, (m + 1) * DIFF_D) for m in range(2)]
    scores = [[_dot_nt(q_ref[rs, cs], k_ref[:, cs]) for cs in maps] for rs in tiles]
    weights = []
    for s0, s1 in scores:
        p0 = jnp.exp2(s0 - jnp.max(s0, axis=-1, keepdims=True))
        p1 = jnp.exp2(s1 - jnp.max(s1, axis=-1, keepdims=True))
        r0 = 1.0 / jnp.sum(p0, axis=-1, keepdims=True)
        r1 = 1.0 / jnp.sum(p1, axis=-1, keepdims=True)
        weights.append((p0 * r0 - p1 * (lam * r1)).astype(BF16))
    outs = [_dot(a, v_ref[...]) for a in weights]
    for rs, o in zip(tiles, outs):
        y = _rms(o, 1e-5) * dn_ref[...] * (1.0 - lambda_init)
        o_ref[rs, :] = y.astype(o_ref.dtype)


def diff_attention(q, k, v, v_block, lam_vecs, diff_norm, lambda_init, tq=512, kv_rows=None):
    bsz, seq, width = q.shape
    rows = k.shape[1] if kv_rows is None else kv_rows
    tq = min(tq, seq)
    hw = 2 * DIFF_D
    kern = functools.partial(_diff_attn_kernel, lambda_init=lambda_init)
    kv = pl.BlockSpec((None, rows, hw), lambda b, h, i: (b, 0, h))
    return pl.pallas_call(
        kern,
        grid=(bsz, DIFF_HEADS, seq // tq),
        in_specs=[pl.BlockSpec((None, tq, hw), lambda b, h, i: (b, i, h)), kv,
                  pl.BlockSpec((None, rows, hw), lambda b, h, i: (b, 0, v_block + h)),
                  pl.BlockSpec((4, DIFF_D), lambda b, h, i: (0, 0)),
                  pl.BlockSpec((1, hw), lambda b, h, i: (0, 0))],
        out_specs=pl.BlockSpec((None, tq, hw), lambda b, h, i: (b, i, h)),
        out_shape=jax.ShapeDtypeStruct((bsz, seq, width), BF16),
        compiler_params=_params("parallel", "parallel", "parallel"),
        name="diff_attention",
    )(q, k, v, lam_vecs, diff_norm.reshape(1, hw))


def _conf_conv_kernel(a_ref, b_ref, w_ref, bias_ref, o_ref, buf_ref):
    rows = a_ref.shape[0]
    zeros = jnp.zeros((CONF_PAD, LANES), F32)
    buf_ref[0:CONF_PAD, :] = zeros
    buf_ref[CONF_PAD + rows:2 * CONF_PAD + rows, :] = zeros
    buf_ref[CONF_PAD:CONF_PAD + rows, :] = a_ref[...].astype(F32) * _sigmoid(b_ref[...].astype(F32))
    w = w_ref[...]
    bias = bias_ref[...]
    first = CONF_PAD - (CONF_K - 1) // 2

    def tile(t, carry):
        base = pl.multiple_of(t * ROW_TILE, ROW_TILE)
        acc = jnp.broadcast_to(bias, (ROW_TILE, LANES))
        for kk in range(CONF_K):
            acc = acc + w[kk:kk + 1, :] * buf_ref[pl.ds(base + first + kk, ROW_TILE), :]
        o_ref[pl.ds(base, ROW_TILE), :] = acc
        return carry

    lax.fori_loop(0, rows // ROW_TILE, tile, 0)


def conf_conv(p, dw, dw_b):
    bsz, rows, _ = p.shape
    ch = dw.shape[1]
    nb = ch // LANES
    return pl.pallas_call(
        _conf_conv_kernel,
        grid=(bsz, nb),
        in_specs=[pl.BlockSpec((None, rows, LANES), lambda b, j: (b, 0, j)),
                  pl.BlockSpec((None, rows, LANES), lambda b, j: (b, 0, nb + j)),
                  pl.BlockSpec((CONF_K, LANES), lambda b, j: (0, j)),
                  pl.BlockSpec((1, LANES), lambda b, j: (0, j))],
        out_specs=pl.BlockSpec((None, rows, LANES), lambda b, j: (b, 0, j)),
        out_shape=jax.ShapeDtypeStruct((bsz, rows, ch), F32),
        scratch_shapes=[pltpu.VMEM((rows + 2 * CONF_PAD, LANES), F32)],
        compiler_params=_params("parallel", "parallel"),
        name="conf_conv",
    )(p, p, dw, dw_b.reshape(1, ch))


def _ln_silu_kernel(x_ref, g_ref, b_ref, o_ref):
    x = x_ref[...]
    xc = x - jnp.mean(x, axis=-1, keepdims=True)
    y = xc * lax.rsqrt(jnp.mean(xc * xc, axis=-1, keepdims=True) + 1e-5) * g_ref[...] + b_ref[...]
    o_ref[...] = _silu(y).astype(o_ref.dtype)


def ln_silu(x, g, b):
    bsz, rows, ch = x.shape
    row = pl.BlockSpec((None, ROW_TILE, ch), lambda bb, t: (bb, t, 0))
    vec = pl.BlockSpec((1, ch), lambda bb, t: (0, 0))
    return pl.pallas_call(
        _ln_silu_kernel,
        grid=(bsz, rows // ROW_TILE),
        in_specs=[row, vec, vec],
        out_specs=row,
        out_shape=jax.ShapeDtypeStruct((bsz, rows, ch), BF16),
        compiler_params=_params("parallel", "parallel"),
        name="ln_silu",
    )(x, g.reshape(1, ch), b.reshape(1, ch))


def _short_conv_kernel(gb_ref, gc_ref, sh_ref, w_ref, o_ref):
    m = gc_ref[...].astype(F32) * sh_ref[...].astype(F32)
    rows = m.shape[0]
    row = lax.broadcasted_iota(jnp.int32, m.shape, 0)
    prev = jnp.where(row == 0, 0.0, pltpu.roll(m, 1, 0))
    nxt = jnp.where(row == rows - 1, 0.0, pltpu.roll(m, rows - 1, 0))
    w = w_ref[...]
    o_ref[...] = (gb_ref[...].astype(F32) * (prev * w[0:1] + m * w[1:2] + nxt * w[2:3])).astype(o_ref.dtype)


def short_conv(p, first_block, w):
    bsz, rows, _ = p.shape
    ch = w.shape[1]
    nb = ch // LANES
    blk = lambda off: pl.BlockSpec((None, rows, LANES), lambda b, j: (b, 0, first_block + off + j))
    return pl.pallas_call(
        _short_conv_kernel,
        grid=(bsz, nb),
        in_specs=[blk(0), blk(nb), blk(2 * nb), pl.BlockSpec((w.shape[0], LANES), lambda b, j: (0, j))],
        out_specs=pl.BlockSpec((None, rows, LANES), lambda b, j: (b, 0, j)),
        out_shape=jax.ShapeDtypeStruct((bsz, rows, ch), BF16),
        compiler_params=_params("parallel", "parallel"),
        name="short_conv",
    )(p, p, p, w)


def _ffn(xn, w_gate, w_up, w_down, layer):
    pad = -w_gate.shape[2] % FF_ALIGN
    wg = cast_weight(w_gate, layer, pad_cols=pad)
    wu = cast_weight(w_up, layer, pad_cols=pad)
    wd = cast_weight(w_down, layer, pad_rows=pad)
    act = ff_up(xn, wg, wu)
    return matmul_ksplit(act, wd, k_steps=4)


def _mixer_even(xn_all, n_ctx, seq, w_in_all, w_out_all, conv_w, a_log, dt_bias, gdn_norm, lam_vecs, diff_norm,
                lambda_init):
    bsz, rows, d = xn_all.shape
    qk_w = GDN_HEADS * GDN_D
    c_ab = 3 * qk_w + qk_w
    n_ab = 4 * GDN_HEADS
    w_in = w_in_all[0].astype(BF16)
    w_ab = jnp.pad(w_in[:, c_ab:c_ab + n_ab], ((0, 0), (0, LANES - n_ab)))
    x2 = xn_all.reshape(bsz * rows, d)
    pa = matmul(x2, w_in, BF16, n_cols=c_ab).reshape(bsz, rows, -1)
    pb = matmul(x2, w_in[:, c_ab + n_ab:], BF16).reshape(bsz, rows, -1)
    ab = matmul(x2, w_ab, F32, tn=LANES).reshape(bsz, rows, LANES)
    qkv = gdn_prep(pa, conv_w, n_ctx)
    gates = gdn_gates(ab, a_log, dt_bias)
    o_f, o_b, c_f, c_b = gdn_scan(qkv, gates, n_ctx)
    z_block = c_ab // (DIFF_HEADS * 2 * DIFF_D) - 1
    ya = gdn_out(o_f, o_b, pa, z_block, n_ctx // ROW_TILE, gdn_norm)
    qr, kr, qc = rope_qk(pb, n_ctx, seq)
    yb = diff_attention(qr, kr, pb, 2 * DIFF_HEADS, lam_vecs, diff_norm, lambda_init)
    w_out = cast_weight(w_out_all, 0)
    y = matmul_cat(ya.reshape(bsz * seq, -1), yb.reshape(bsz * seq, -1), w_out)
    ya_c = gdn_out(c_f, c_b, pa, z_block, 0, gdn_norm)
    yb_c = diff_attention(qc, kr, pb, 2 * DIFF_HEADS, lam_vecs, diff_norm, lambda_init, kv_rows=n_ctx)
    y_c = matmul_cat(ya_c.reshape(bsz * n_ctx, -1), yb_c.reshape(bsz * n_ctx, -1), w_out)
    return y.reshape(bsz, seq, d), y_c.reshape(bsz, n_ctx, d)


def _mixer_odd(xn, w_in_all, w_out_all, conf_dw, conf_dw_b, conf_ln_g, conf_ln_b, sc_w):
    bsz, seq, d = xn.shape
    p = matmul(xn.reshape(bsz * seq, d), cast_weight(w_in_all, 0), BF16).reshape(bsz, seq, -1)
    yc = ln_silu(conf_conv(p, conf_dw, conf_dw_b), conf_ln_g, conf_ln_b)
    yd = short_conv(p, 2 * (conf_dw.shape[1] // LANES), sc_w)
    y = matmul_cat(yc.reshape(bsz * seq, -1), yd.reshape(bsz * seq, -1), cast_weight(w_out_all, 0))
    return y.reshape(bsz, seq, d)


@jax.jit
def _forward(x, c, ctx, c_ctx, w_ada, b_ada, g_mix_pre, g_mix_post, g_ffn_pre, g_ffn_post,
             w_ff_gate, w_ff_up, w_ff_down, w_in_even, w_out_even, gdn_conv, gdn_a_log,
             gdn_dt_bias, gdn_norm, diff_lambda, diff_norm, w_in_odd, w_out_odd, conf_dw,
             conf_dw_b, conf_ln_g, conf_ln_b, sc_conv):
    bsz, seq, d = x.shape
    n_ctx = ctx.shape[1]
    depth = w_ada.shape[0]
    assert depth == 2, "the context stream is only carried through the first (even) layer"
    cond = jnp.concatenate([c, c_ctx[None, :], jnp.zeros((SUBLANES - bsz - 1, d), F32)], axis=0)
    mods = [adaln(cond, w_ada, b_ada, i).reshape(SUBLANES, 6, 1, d) for i in range(depth)]

    xn_all = prenorm_with_ctx(x, ctx, g_mix_pre[0], mods[0], bsz)
    y, y_c = _mixer_even(xn_all, n_ctx, seq, w_in_even, w_out_even, gdn_conv[0], gdn_a_log[0],
                         gdn_dt_bias[0], gdn_norm[0], diff_lambda[0], diff_norm[0], 0.8 - 0.6 * math.exp(-0.3 * 0))
    x, xn = postnorm(x, y, mods[0], 2, g_mix_post[0], (g_ffn_pre[0], mods[0], 3, 4))
    f = _ffn(xn.reshape(bsz * seq, d), w_ff_gate, w_ff_up, w_ff_down, 0).reshape(bsz, seq, d)
    x, xn = postnorm(x, f, mods[0], 5, g_ffn_post[0], (g_mix_pre[1], mods[1], 0, 1))
    h, hn = postnorm(ctx, y_c, mods[0], 2, g_mix_post[0], (g_ffn_pre[0], mods[0], 3, 4), mod_row=bsz)
    f_c = _ffn(hn.reshape(bsz * n_ctx, d), w_ff_gate, w_ff_up, w_ff_down, 0).reshape(bsz, n_ctx, d)
    h = postnorm(h, f_c, mods[0], 5, g_ffn_post[0], mod_row=bsz)
    del h

    y = _mixer_odd(xn, w_in_odd, w_out_odd, conf_dw[0], conf_dw_b[0], conf_ln_g[0], conf_ln_b[0],
                   sc_conv[0])
    x, xn = postnorm(x, y, mods[1], 2, g_mix_post[1], (g_ffn_pre[1], mods[1], 3, 4))
    f = _ffn(xn.reshape(bsz * seq, d), w_ff_gate, w_ff_up, w_ff_down, 1).reshape(bsz, seq, d)
    return postnorm(x, f, mods[1], 5, g_ffn_post[1])


def kernel(x, c, ctx, c_ctx, w_ada, b_ada, g_mix_pre, g_mix_post, g_ffn_pre, g_ffn_post, w_ff_gate, w_ff_up,
           w_ff_down, w_in_even, w_out_even, gdn_conv, gdn_a_log, gdn_dt_bias, gdn_norm, diff_lambda, diff_norm,
           w_in_odd, w_out_odd, conf_dw, conf_dw_b, conf_ln_g, conf_ln_b, sc_conv):
    return _forward(x, c, ctx, c_ctx, w_ada, b_ada, g_mix_pre, g_mix_post, g_ffn_pre, g_ffn_post, w_ff_gate,
                    w_ff_up, w_ff_down, w_in_even, w_out_even, gdn_conv, gdn_a_log, gdn_dt_bias, gdn_norm,
                    diff_lambda, diff_norm, w_in_odd, w_out_odd, conf_dw, conf_dw_b, conf_ln_g, conf_ln_b,
                    sc_conv)
```

```python
import functools
import math

import jax
import jax.numpy as jnp
from jax import lax
from jax.experimental import pallas as pl
from jax.experimental.pallas import tpu as pltpu

F32 = jnp.float32
BF16 = jnp.bfloat16

GRID_W = 64
GDN_HEADS = 16
GDN_D = 128
GDN_CHUNK = 64
DIFF_HEADS = 8
DIFF_D = 128
ROPE_BASE = 10000.0
CONF_K = 31
CONF_PAD = 16

LANES = 128
SUBLANES = 8
VMEM_LIMIT = 56 * 1024 * 1024
ROW_TILE = 256
MM_TILE = 1024
FF_TILE = 512
FF_ALIGN = 1024
ATTN_SUBTILES = 2


def _params(*sem):
    return pltpu.CompilerParams(dimension_semantics=sem, vmem_limit_bytes=VMEM_LIMIT)


def _sigmoid(x):
    return 1.0 / (1.0 + jnp.exp(-x))


def _silu(x):
    return x * _sigmoid(x)


def _dot(a, b):
    return jnp.dot(a, b, preferred_element_type=F32)


def _dot_nt(a, b):
    return lax.dot_general(a, b, (((1,), (1,)), ((), ())), preferred_element_type=F32)


def _dot_tn(a, b):
    return lax.dot_general(a, b, (((0,), (0,)), ((), ())), preferred_element_type=F32)


def _hi_lo(x):
    hi = x.astype(BF16)
    return hi, (x - hi.astype(F32)).astype(BF16)


def _split_lhs(a):
    hi, lo = _hi_lo(a)
    return jnp.concatenate([hi, hi, lo], axis=1)


def _split_rhs(b):
    hi, lo = _hi_lo(b)
    return jnp.concatenate([hi, lo, hi], axis=0)


def _adaln_kernel(c_ref, w_ref, b_ref, o_ref):
    s = _silu(c_ref[...]).astype(BF16)
    o_ref[...] = _dot(s, w_ref[...].astype(BF16)) + b_ref[...]


def adaln(cond, w, b, layer):
    rows, d = cond.shape
    depth, _, n = w.shape
    tn = 512
    return pl.pallas_call(
        _adaln_kernel,
        grid=(n // tn,),
        in_specs=[pl.BlockSpec((rows, d), lambda j: (0, 0)),
                  pl.BlockSpec((None, d, tn), lambda j: (layer, 0, j)),
                  pl.BlockSpec((None, 1, tn), lambda j: (layer, 0, j))],
        out_specs=pl.BlockSpec((rows, tn), lambda j: (0, j)),
        out_shape=jax.ShapeDtypeStruct((rows, n), F32),
        compiler_params=_params("parallel"),
        name="adaln",
    )(cond, w, b.reshape(depth, 1, n))


def _rms(x, eps):
    return x * lax.rsqrt(jnp.mean(x * x, axis=-1, keepdims=True) + eps)


def _prenorm_ctx_kernel(x_ref, c_ref, g_ref, sh_ref, sc_ref, o_ref):
    xv = jnp.where(pl.program_id(1) == 0, c_ref[...], x_ref[...])
    y = _rms(xv, 1e-6) * g_ref[...]
    o_ref[...] = (y * (1.0 + sc_ref[...]) + sh_ref[...]).astype(o_ref.dtype)


def prenorm_with_ctx(x, ctx, g, mods, ctx_row):
    bsz, seq, d = x.shape
    n_ctx = ctx.shape[1]
    assert n_ctx == ROW_TILE
    nt = (seq + n_ctx) // ROW_TILE
    mrow = lambda b, t: jnp.where(t == 0, ctx_row, b)
    return pl.pallas_call(
        _prenorm_ctx_kernel,
        grid=(bsz, nt),
        in_specs=[pl.BlockSpec((None, ROW_TILE, d), lambda b, t: (b, jnp.maximum(t - 1, 0), 0)),
                  pl.BlockSpec((None, ROW_TILE, d), lambda b, t: (b, 0, 0)),
                  pl.BlockSpec((1, d), lambda b, t: (0, 0)),
                  pl.BlockSpec((None, None, 1, d), lambda b, t: (mrow(b, t), 0, 0, 0)),
                  pl.BlockSpec((None, None, 1, d), lambda b, t: (mrow(b, t), 1, 0, 0))],
        out_specs=pl.BlockSpec((None, ROW_TILE, d), lambda b, t: (b, t, 0)),
        out_shape=jax.ShapeDtypeStruct((bsz, seq + n_ctx, d), BF16),
        compiler_params=_params("parallel", "parallel"),
        name="prenorm_ctx",
    )(x, ctx, g.reshape(1, d), mods, mods)


def _postnorm_kernel(x_ref, y_ref, gate_ref, gpost_ref, gpre_ref, sh_ref, sc_ref, xo_ref, no_ref):
    xn = x_ref[...] + gate_ref[...] * (_rms(y_ref[...].astype(F32), 1e-6) * gpost_ref[...])
    xo_ref[...] = xn
    h = _rms(xn, 1e-6) * gpre_ref[...]
    no_ref[...] = (h * (1.0 + sc_ref[...]) + sh_ref[...]).astype(no_ref.dtype)


def _postnorm_last_kernel(x_ref, y_ref, gate_ref, gpost_ref, xo_ref):
    xo_ref[...] = x_ref[...] + gate_ref[...] * (_rms(y_ref[...].astype(F32), 1e-6) * gpost_ref[...])


def postnorm(x, y, mods, gate_idx, g_post, nxt=None, mod_row=None):
    bsz, seq, d = x.shape
    row = pl.BlockSpec((None, ROW_TILE, d), lambda b, t: (b, t, 0))
    vec = pl.BlockSpec((1, d), lambda b, t: (0, 0))
    mod = lambda k: pl.BlockSpec((None, None, 1, d), lambda b, t: (b if mod_row is None else mod_row, k, 0, 0))
    grid = (bsz, seq // ROW_TILE)
    if nxt is None:
        return pl.pallas_call(
            _postnorm_last_kernel, grid=grid,
            in_specs=[row, row, mod(gate_idx), vec],
            out_specs=row,
            out_shape=jax.ShapeDtypeStruct((bsz, seq, d), F32),
            compiler_params=_params("parallel", "parallel"),
            name="postnorm_last",
        )(x, y, mods, g_post.reshape(1, d))
    g_pre, mods_n, shift_idx, scale_idx = nxt
    return pl.pallas_call(
        _postnorm_kernel, grid=grid,
        in_specs=[row, row, mod(gate_idx), vec, vec, mod(shift_idx), mod(scale_idx)],
        out_specs=[row, row],
        out_shape=[jax.ShapeDtypeStruct((bsz, seq, d), F32), jax.ShapeDtypeStruct((bsz, seq, d), BF16)],
        compiler_params=_params("parallel", "parallel"),
        name="postnorm",
    )(x, y, mods, g_post.reshape(1, d), g_pre.reshape(1, d), mods_n, mods_n)


def _cast_pad_kernel(w_ref, o_ref, *, n_row_tiles):
    cols = w_ref.shape[1]

    @pl.when(pl.program_id(0) < n_row_tiles)
    def _():
        o_ref[:, :cols] = w_ref[...].astype(o_ref.dtype)
        if o_ref.shape[1] > cols:
            o_ref[:, cols:] = jnp.zeros((o_ref.shape[0], o_ref.shape[1] - cols), o_ref.dtype)

    @pl.when(pl.program_id(0) >= n_row_tiles)
    def _():
        o_ref[...] = jnp.zeros(o_ref.shape, o_ref.dtype)


def cast_weight(w, layer, n_cols=None, pad_rows=0, pad_cols=0):
    _, rows, cols = w.shape
    cols = cols if n_cols is None else n_cols
    assert rows % ROW_TILE == 0 and pad_rows % ROW_TILE == 0
    assert pad_cols == 0 or (cols % LANES == 0 and pad_cols % LANES == 0)
    n_row_tiles = rows // ROW_TILE
    kern = functools.partial(_cast_pad_kernel, n_row_tiles=n_row_tiles)
    return pl.pallas_call(
        kern,
        grid=((rows + pad_rows) // ROW_TILE,),
        in_specs=[pl.BlockSpec((None, ROW_TILE, cols), lambda i: (layer, jnp.minimum(i, n_row_tiles - 1), 0))],
        out_specs=pl.BlockSpec((ROW_TILE, cols + pad_cols), lambda i: (i, 0)),
        out_shape=jax.ShapeDtypeStruct((rows + pad_rows, cols + pad_cols), BF16),
        compiler_params=_params("parallel"),
        name="cast_weight",
    )(w)


def _mm_kernel(x_ref, w_ref, o_ref):
    o_ref[...] = _dot(x_ref[...], w_ref[...]).astype(o_ref.dtype)


def matmul(x, w, out_dtype, tn=MM_TILE, n_cols=None):
    m, k = x.shape
    n = w.shape[1] if n_cols is None else n_cols
    tm = MM_TILE
    return pl.pallas_call(
        _mm_kernel,
        grid=(m // tm, n // tn),
        in_specs=[pl.BlockSpec((tm, k), lambda i, j: (i, 0)),
                  pl.BlockSpec((k, tn), lambda i, j: (0, j))],
        out_specs=pl.BlockSpec((tm, tn), lambda i, j: (i, j)),
        out_shape=jax.ShapeDtypeStruct((m, n), out_dtype),
        compiler_params=_params("parallel", "parallel"),
        name="matmul",
    )(x, w)


def _mm2_kernel(a_ref, b_ref, w_ref, o_ref):
    ka = a_ref.shape[1]
    o_ref[...] = (_dot(a_ref[...], w_ref[:ka, :]) + _dot(b_ref[...], w_ref[ka:, :])).astype(o_ref.dtype)


def matmul_cat(a, b, w):
    m, ka = a.shape
    kb = b.shape[1]
    n = w.shape[1]
    tm = tn = MM_TILE
    return pl.pallas_call(
        _mm2_kernel,
        grid=(m // tm, n // tn),
        in_specs=[pl.BlockSpec((tm, ka), lambda i, j: (i, 0)),
                  pl.BlockSpec((tm, kb), lambda i, j: (i, 0)),
                  pl.BlockSpec((ka + kb, tn), lambda i, j: (0, j))],
        out_specs=pl.BlockSpec((tm, tn), lambda i, j: (i, j)),
        out_shape=jax.ShapeDtypeStruct((m, n), BF16),
        compiler_params=_params("parallel", "parallel"),
        name="matmul_cat",
    )(a, b, w)


def _ff_up_kernel(x_ref, wg_ref, wu_ref, o_ref):
    x = x_ref[...]
    o_ref[...] = (_silu(_dot(x, wg_ref[...])) * _dot(x, wu_ref[...])).astype(o_ref.dtype)


def ff_up(x, wg, wu):
    m, k = x.shape
    n = wg.shape[1]
    tm, tn = MM_TILE, FF_TILE
    wspec = pl.BlockSpec((k, tn), lambda i, j: (0, j))
    return pl.pallas_call(
        _ff_up_kernel,
        grid=(m // tm, n // tn),
        in_specs=[pl.BlockSpec((tm, k), lambda i, j: (i, 0)), wspec, wspec],
        out_specs=pl.BlockSpec((tm, tn), lambda i, j: (i, j)),
        out_shape=jax.ShapeDtypeStruct((m, n), BF16),
        compiler_params=_params("parallel", "parallel"),
        name="ff_up",
    )(x, wg, wu)


def _mm_acc_kernel(x_ref, w_ref, o_ref, acc_ref):
    part = _dot(x_ref[...], w_ref[...])
    kk = pl.program_id(2)
    last = pl.num_programs(2) - 1

    @pl.when(kk == 0)
    def _():
        acc_ref[...] = part

    @pl.when((kk > 0) & (kk < last))
    def _():
        acc_ref[...] += part

    @pl.when(kk == last)
    def _():
        o_ref[...] = (acc_ref[...] + part).astype(o_ref.dtype)


def matmul_ksplit(x, w, k_steps, tm=MM_TILE):
    m, k = x.shape
    n = w.shape[1]
    tm = min(tm, m)
    tn = MM_TILE
    tk = k // k_steps
    assert k_steps >= 2 and tk * k_steps == k and tk % LANES == 0
    return pl.pallas_call(
        _mm_acc_kernel,
        grid=(m // tm, n // tn, k_steps),
        in_specs=[pl.BlockSpec((tm, tk), lambda i, j, kk: (i, kk)),
                  pl.BlockSpec((tk, tn), lambda i, j, kk: (kk, j))],
        out_specs=pl.BlockSpec((tm, tn), lambda i, j, kk: (i, j)),
        out_shape=jax.ShapeDtypeStruct((m, n), BF16),
        scratch_shapes=[pltpu.VMEM((tm, tn), F32)],
        compiler_params=_params("parallel", "parallel", "arbitrary"),
        name="matmul_ksplit",
    )(x, w)


def _gdn_prep_kernel(p_ref, w_ref, o_ref, *, n_ctx, n_norm_blocks):
    x = p_ref[...].astype(F32)
    rows = x.shape[0]
    row = lax.broadcasted_iota(jnp.int32, x.shape, 0)
    prev = jnp.where((row == 0) | (row == n_ctx), 0.0, pltpu.roll(x, 1, 0))
    nxt = jnp.where((row == n_ctx - 1) | (row == rows - 1), 0.0, pltpu.roll(x, rows - 1, 0))
    w = w_ref[...]
    y = _silu(prev * w[0:1] + x * w[1:2] + nxt * w[2:3])
    yn = y * lax.rsqrt(jnp.sum(y * y, axis=-1, keepdims=True) + 1e-6)
    o_ref[...] = jnp.where(pl.program_id(1) < n_norm_blocks, yn, y).astype(o_ref.dtype)


def gdn_prep(p, conv_w, n_ctx):
    bsz, rows, _ = p.shape
    n_blocks = conv_w.shape[1] // LANES
    kern = functools.partial(_gdn_prep_kernel, n_ctx=n_ctx, n_norm_blocks=2 * GDN_HEADS)
    return pl.pallas_call(
        kern,
        grid=(bsz, n_blocks),
        in_specs=[pl.BlockSpec((None, rows, LANES), lambda b, j: (b, 0, j)),
                  pl.BlockSpec((conv_w.shape[0], LANES), lambda b, j: (0, j))],
        out_specs=pl.BlockSpec((None, rows, LANES), lambda b, j: (b, 0, j)),
        out_shape=jax.ShapeDtypeStruct((bsz, rows, n_blocks * LANES), BF16),
        compiler_params=_params("parallel", "parallel"),
        name="gdn_prep",
    )(p, conv_w)


def _gdn_gates_kernel(ab_ref, alog_ref, dt_ref, o_ref):
    x = ab_ref[...]
    rows = x.shape[0]
    lane = lax.broadcasted_iota(jnp.int32, x.shape, 1)
    r64 = lax.broadcasted_iota(jnp.int32, x.shape, 0) & (GDN_CHUNK - 1)
    z = x + dt_ref[...]
    g = -jnp.exp(alog_ref[...]) * (jnp.maximum(z, 0.0) + jnp.log1p(jnp.exp(-jnp.abs(z))))
    pre = g
    suf = g
    s = 1
    while s < GDN_CHUNK:
        pre = pre + jnp.where(r64 >= s, pltpu.roll(pre, s, 0), 0.0)
        suf = suf + jnp.where(r64 < GDN_CHUNK - s, pltpu.roll(suf, rows - s, 0), 0.0)
        s *= 2
    val = jnp.where(lane < GDN_HEADS, pre, jnp.where(lane < 2 * GDN_HEADS, suf, _sigmoid(x)))
    for j in range(4 * GDN_HEADS):
        o_ref[j] = jnp.broadcast_to(val[:, j:j + 1], (rows, LANES))


def gdn_gates(ab, a_log, dt_bias):
    bsz, rows, _ = ab.shape
    pad = jnp.zeros((LANES - 2 * GDN_HEADS,), F32)
    alog = jnp.concatenate([a_log.reshape(-1), pad]).reshape(1, LANES)
    dt = jnp.concatenate([dt_bias.reshape(-1), pad]).reshape(1, LANES)
    return pl.pallas_call(
        _gdn_gates_kernel,
        grid=(bsz, rows // ROW_TILE),
        in_specs=[pl.BlockSpec((None, ROW_TILE, LANES), lambda b, t: (b, t, 0)),
                  pl.BlockSpec((1, LANES), lambda b, t: (0, 0)),
                  pl.BlockSpec((1, LANES), lambda b, t: (0, 0))],
        out_specs=pl.BlockSpec((None, 4 * GDN_HEADS, ROW_TILE, LANES), lambda b, t: (b, 0, t, 0)),
        out_shape=jax.ShapeDtypeStruct((bsz, 4 * GDN_HEADS, rows, LANES), F32),
        compiler_params=_params("parallel", "parallel"),
        name="gdn_gates",
    )(ab, alog, dt)


def _gdn_chains(chains):
    c = GDN_CHUNK
    n_chunks = chains[0][0].shape[0] // c
    r_iota = lax.broadcasted_iota(jnp.int32, (c, c), 0)
    c_iota = lax.broadcasted_iota(jnp.int32, (c, c), 1)
    eye = (r_iota == c_iota).astype(F32)
    masks = {True: (r_iota >= c_iota, r_iota > c_iota), False: (r_iota <= c_iota, r_iota < c_iota)}

    items = []
    for (q, k, v, g, beta, _, fwd) in chains:
        incl, strict = masks[fwd]
        for n in range(n_chunks):
            rs = slice(n * c, (n + 1) * c)
            qn = q[rs, :] * (GDN_D ** -0.5)
            kn, gn, bn = k[rs, :], g[rs, :], beta[rs, :]
            g_cols = gn.T[:c, :]
            decay = jnp.where(incl, jnp.exp(jnp.where(incl, gn[:, :c] - g_cols, 0.0)), 0.0)
            kb = kn * bn
            eg = jnp.exp(gn)
            g_last = gn[c - 1:c, :] if fwd else gn[0:1, :]
            items.append(dict(
                incl=incl, strict=strict, decay=decay, kbf=kn.astype(BF16),
                lhs=jnp.concatenate([kb, qn], axis=0).astype(BF16),
                rhs=jnp.concatenate([v[rs, :] * bn, kb * eg], axis=1),
                qd=qn * eg, kt=(kn * jnp.exp(g_last - gn)).astype(BF16), e_last=jnp.exp(g_last)))
    for it in items:
        it["both"] = _dot_nt(it["lhs"], it["kbf"])
    for it in items:
        lmat = jnp.where(it["strict"], it["both"][:c, :] * it["decay"], 0.0)
        it["qk"] = jnp.where(it["incl"], it["both"][c:, :] * it["decay"], 0.0).astype(BF16)
        it["tinv"] = eye - lmat
        it["pw"] = lmat
    for it in items:
        it["pw"] = _dot(_split_lhs(it["pw"]), _split_rhs(it["pw"]))
    for stage in range(5):
        for it in items:
            if stage < 4:
                prod = _dot(_split_lhs(it["pw"]), _split_rhs(jnp.concatenate([it["tinv"], it["pw"]], axis=1)))
                it["tinv"] = it["tinv"] + prod[:, :c]
                it["pw"] = prod[:, c:]
            else:
                it["tinv"] = it["tinv"] + _dot(_split_lhs(it["tinv"]), _split_rhs(it["pw"]))
    for it in items:
        sol = _dot(_split_lhs(it["tinv"]), _split_rhs(it["rhs"]))
        it["u"] = sol[:, :GDN_D]
        it["wq"] = jnp.concatenate([sol[:, GDN_D:], it["qd"]], axis=0).astype(BF16)

    states = [ch[5] for ch in chains]
    outs = [[None] * n_chunks for _ in chains]
    for step in range(n_chunks):
        cur = [items[ci * n_chunks + (step if ch[6] else n_chunks - 1 - step)] for ci, ch in enumerate(chains)]
        ws = [_dot(it["wq"], s.astype(BF16)) for it, s in zip(cur, states)]
        vb = [(it["u"] - w[:c, :]).astype(BF16) for it, w in zip(cur, ws)]
        for ci, (ch, it) in enumerate(zip(chains, cur)):
            n = step if ch[6] else n_chunks - 1 - step
            outs[ci][n] = ws[ci][c:, :] + _dot(it["qk"], vb[ci])
            states[ci] = states[ci] * it["e_last"] + _dot_tn(it["kt"], vb[ci])
    return [(jnp.concatenate(o, axis=0), s) for o, s in zip(outs, states)]


def _gdn_kernel(qf_ref, kf_ref, vf_ref, qb_ref, kb_ref, vb_ref, gf_ref, bf_ref, gb_ref, bb_ref,
                of_ref, ob_ref, cf_ref, cb_ref, s_ref):
    t = pl.program_id(2)

    @pl.when(t == 0)
    def _():
        s_ref[...] = jnp.zeros(s_ref.shape, F32)

    dirs = ((qf_ref, kf_ref, vf_ref, gf_ref, bf_ref, of_ref, cf_ref),
            (qb_ref, kb_ref, vb_ref, gb_ref, bb_ref, ob_ref, cb_ref))
    chains, dests = [], []
    for hh in range(gf_ref.shape[0]):
        cs = slice(hh * GDN_D, (hh + 1) * GDN_D)
        for d, (q_ref, k_ref, v_ref, g_ref, b_ref, o_ref, c_ref) in enumerate(dirs):
            chains.append((q_ref[:, cs].astype(F32), k_ref[:, cs].astype(F32), v_ref[:, cs].astype(F32),
                           g_ref[hh], b_ref[hh], s_ref[2 * hh + d], d == 0))
            dests.append((o_ref, c_ref, cs, 2 * hh + d))
    results = _gdn_chains(chains)
    for (o_ref, _, cs, idx), (o, s_new) in zip(dests, results):
        s_ref[idx] = s_new
        o_ref[:, cs] = o

    @pl.when(t == 0)
    def _():
        for (_, c_ref, cs, _), (o, _) in zip(dests, results):
            c_ref[:, cs] = o


def gdn_scan(qkv, gates, n_ctx, heads_per_step=8):
    bsz, rows, _ = qkv.shape
    assert n_ctx == ROW_TILE
    h, hb = GDN_HEADS, heads_per_step
    nt = rows // ROW_TILE
    seq = rows - n_ctx
    wid = hb * GDN_D
    tile_f = lambda t: t
    tile_b = lambda t: jnp.where(t == 0, 0, nt - t)
    out_f = lambda t: jnp.maximum(t - 1, 0)
    out_b = lambda t: jnp.where(t == 0, nt - 2, nt - 1 - t)
    col = lambda part, tile: pl.BlockSpec((None, ROW_TILE, wid), lambda b, hg, t: (b, tile(t), part * (h // hb) + hg))
    gate = lambda kind, tile: pl.BlockSpec((None, hb, ROW_TILE, LANES),
                                           lambda b, hg, t: (b, kind * (h // hb) + hg, tile(t), 0))
    out = lambda tile: pl.BlockSpec((None, ROW_TILE, wid), lambda b, hg, t: (b, tile(t), hg))
    shp = jax.ShapeDtypeStruct((bsz, seq, h * GDN_D), F32)
    shp_c = jax.ShapeDtypeStruct((bsz, n_ctx, h * GDN_D), F32)
    ctx_tile = lambda t: 0
    return pl.pallas_call(
        _gdn_kernel,
        grid=(bsz, h // hb, nt),
        in_specs=[col(0, tile_f), col(1, tile_f), col(2, tile_f), col(0, tile_b), col(1, tile_b), col(2, tile_b),
                  gate(0, tile_f), gate(2, tile_f), gate(1, tile_b), gate(3, tile_b)],
        out_specs=[out(out_f), out(out_b), out(ctx_tile), out(ctx_tile)],
        out_shape=[shp, shp, shp_c, shp_c],
        scratch_shapes=[pltpu.VMEM((2 * hb, GDN_D, GDN_D), F32)],
        compiler_params=_params("parallel", "parallel", "arbitrary"),
        name="gdn_scan",
    )(qkv, qkv, qkv, qkv, qkv, qkv, gates, gates, gates, gates)


def _gdn_out_kernel(of_ref, ob_ref, z_ref, gn_ref, y_ref):
    gn = gn_ref[...]
    for hh in range(GDN_HEADS):
        cs = slice(hh * GDN_D, (hh + 1) * GDN_D)
        o = of_ref[:, cs] + ob_ref[:, cs]
        y_ref[:, cs] = (_rms(o, 1e-6) * gn * _silu(z_ref[:, cs].astype(F32))).astype(y_ref.dtype)


def gdn_out(o_f, o_b, p, z_block, row_off_blocks, gn):
    bsz, seq, width = o_f.shape
    row = pl.BlockSpec((None, ROW_TILE, width), lambda b, t: (b, t, 0))
    return pl.pallas_call(
        _gdn_out_kernel,
        grid=(bsz, seq // ROW_TILE),
        in_specs=[row, row,
                  pl.BlockSpec((None, ROW_TILE, width), lambda b, t: (b, t + row_off_blocks, z_block)),
                  pl.BlockSpec((1, GDN_D), lambda b, t: (0, 0))],
        out_specs=row,
        out_shape=jax.ShapeDtypeStruct((bsz, seq, width), BF16),
        compiler_params=_params("parallel", "parallel"),
        name="gdn_out",
    )(o_f, o_b, p, gn.reshape(1, GDN_D))


def _rope_kernel(q_ref, k_ref, cos_ref, sa_ref, sb_ref, qo_ref, ko_ref, qc_ref, *, n_ctx_tiles):
    cos, sa, sb = cos_ref[...], sa_ref[...], sb_ref[...]

    def rope(x):
        return x * cos + pltpu.roll(x, LANES - DIFF_D // 4, 1) * sa + pltpu.roll(x, DIFF_D // 4, 1) * sb

    q_scale = (DIFF_D ** -0.5) * math.log2(math.e)
    for j in range(q_ref.shape[1] // DIFF_D):
        cs = slice(j * DIFF_D, (j + 1) * DIFF_D)
        ko_ref[:, cs] = rope(k_ref[:, cs].astype(F32)).astype(ko_ref.dtype)

    def write_q(dst_ref):
        for j in range(q_ref.shape[1] // DIFF_D):
            cs = slice(j * DIFF_D, (j + 1) * DIFF_D)
            dst_ref[:, cs] = rope(q_ref[:, cs].astype(F32) * q_scale).astype(dst_ref.dtype)

    pl.when(pl.program_id(1) >= n_ctx_tiles)(lambda: write_q(qo_ref))
    pl.when(pl.program_id(1) < n_ctx_tiles)(lambda: write_q(qc_ref))


def rope_tables(n_ctx, seq):
    half = DIFF_D // 2
    rows = seq // GRID_W
    inv = 1.0 / (ROPE_BASE ** (jnp.arange(0, half, 2, dtype=F32) / half))
    r = jnp.repeat(jnp.arange(rows), GRID_W).astype(F32)[:, None]
    col = jnp.tile(jnp.arange(GRID_W), rows).astype(F32)[:, None]
    ang = jnp.concatenate([r * inv, r * inv, col * inv, col * inv], axis=-1)
    cos, sin = jnp.cos(ang), jnp.sin(ang)
    first_quarter = (jnp.arange(DIFF_D) % half) < half // 2
    sin_a = jnp.where(first_quarter, -sin, 0.0)
    sin_b = jnp.where(first_quarter, 0.0, sin)
    ones = jnp.ones((n_ctx, DIFF_D), F32)
    zeros = jnp.zeros((n_ctx, DIFF_D), F32)
    cat = lambda a, b: jnp.concatenate([a, b], axis=0)
    return cat(ones, cos), cat(zeros, sin_a), cat(zeros, sin_b)


def rope_qk(p, n_ctx, seq):
    bsz, rows, _ = p.shape
    width = DIFF_HEADS * 2 * DIFF_D
    n_ctx_tiles = n_ctx // ROW_TILE
    cos, sa, sb = rope_tables(n_ctx, seq)
    blk = lambda off: pl.BlockSpec((None, ROW_TILE, width), lambda b, t: (b, t, off))
    tab = pl.BlockSpec((ROW_TILE, DIFF_D), lambda b, t: (t, 0))
    return pl.pallas_call(
        functools.partial(_rope_kernel, n_ctx_tiles=n_ctx_tiles),
        grid=(bsz, rows // ROW_TILE),
        in_specs=[blk(0), blk(1), tab, tab, tab],
        out_specs=[pl.BlockSpec((None, ROW_TILE, width), lambda b, t: (b, jnp.maximum(t - n_ctx_tiles, 0), 0)),
                   pl.BlockSpec((None, ROW_TILE, width), lambda b, t: (b, t, 0)),
                   pl.BlockSpec((None, ROW_TILE, width), lambda b, t: (b, jnp.minimum(t, n_ctx_tiles - 1), 0))],
        out_shape=[jax.ShapeDtypeStruct((bsz, seq, width), BF16), jax.ShapeDtypeStruct((bsz, rows, width), BF16),
                   jax.ShapeDtypeStruct((bsz, n_ctx, width), BF16)],
        compiler_params=_params("parallel", "arbitrary"),
        name="rope_qk",
    )(p, p, cos, sa, sb)


def _diff_attn_kernel(q_ref, k_ref, v_ref, lam_ref, dn_ref, o_ref, *, lambda_init):
    lv = lam_ref[...]
    lam = (jnp.exp(jnp.sum(lv[0:1] * lv[1:2], axis=-1, keepdims=True))
           - jnp.exp(jnp.sum(lv[2:3] * lv[3:4], axis=-1, keepdims=True)) + lambda_init)

    sub = q_ref.shape[0] // ATTN_SUBTILES
    tiles = [slice(i * sub, (i + 1) * sub) for i in range(ATTN_SUBTILES)]
    maps = [slice(m * DIFF_D, (m + 1) * DIFF_D) for m in range(2)]
    scores = [[_dot_nt(q_ref[rs, cs], k_ref[:, cs]) for cs in maps] for rs in tiles]
    weights = []
    for s0, s1 in scores:
        p0 = jnp.exp2(s0 - jnp.max(s0, axis=-1, keepdims=True))
        p1 = jnp.exp2(s1 - jnp.max(s1, axis=-1, keepdims=True))
        r0 = 1.0 / jnp.sum(p0, axis=-1, keepdims=True)
        r1 = 1.0 / jnp.sum(p1, axis=-1, keepdims=True)
        weights.append((p0 * r0 - p1 * (lam * r1)).astype(BF16))
    outs = [_dot(a, v_ref[...]) for a in weights]
    for rs, o in zip(tiles, outs):
        y = _rms(o, 1e-5) * dn_ref[...] * (1.0 - lambda_init)
        o_ref[rs, :] = y.astype(o_ref.dtype)


def diff_attention(q, k, v, v_block, lam_vecs, diff_norm, lambda_init, tq=512, kv_rows=None):
    bsz, seq, width = q.shape
    rows = k.shape[1] if kv_rows is None else kv_rows
    tq = min(tq, seq)
    hw = 2 * DIFF_D
    kern = functools.partial(_diff_attn_kernel, lambda_init=lambda_init)
    kv = pl.BlockSpec((None, rows, hw), lambda b, h, i: (b, 0, h))
    return pl.pallas_call(
        kern,
        grid=(bsz, DIFF_HEADS, seq // tq),
        in_specs=[pl.BlockSpec((None, tq, hw), lambda b, h, i: (b, i, h)), kv,
                  pl.BlockSpec((None, rows, hw), lambda b, h, i: (b, 0, v_block + h)),
                  pl.BlockSpec((4, DIFF_D), lambda b, h, i: (0, 0)),
                  pl.BlockSpec((1, hw), lambda b, h, i: (0, 0))],
        out_specs=pl.BlockSpec((None, tq, hw), lambda b, h, i: (b, i, h)),
        out_shape=jax.ShapeDtypeStruct((bsz, seq, width), BF16),
        compiler_params=_params("parallel", "parallel", "parallel"),
        name="diff_attention",
    )(q, k, v, lam_vecs, diff_norm.reshape(1, hw))


def _conf_conv_kernel(a_ref, b_ref, w_ref, bias_ref, o_ref, buf_ref):
    rows = a_ref.shape[0]
    zeros = jnp.zeros((CONF_PAD, LANES), F32)
    buf_ref[0:CONF_PAD, :] = zeros
    buf_ref[CONF_PAD + rows:2 * CONF_PAD + rows, :] = zeros
    buf_ref[CONF_PAD:CONF_PAD + rows, :] = a_ref[...].astype(F32) * _sigmoid(b_ref[...].astype(F32))
    w = w_ref[...]
    bias = bias_ref[...]
    first = CONF_PAD - (CONF_K - 1) // 2

    def tile(t, carry):
        base = pl.multiple_of(t * ROW_TILE, ROW_TILE)
        acc = jnp.broadcast_to(bias, (ROW_TILE, LANES))
        for kk in range(CONF_K):
            acc = acc + w[kk:kk + 1, :] * buf_ref[pl.ds(base + first + kk, ROW_TILE), :]
        o_ref[pl.ds(base, ROW_TILE), :] = acc
        return carry

    lax.fori_loop(0, rows // ROW_TILE, tile, 0)


def conf_conv(p, dw, dw_b):
    bsz, rows, _ = p.shape
    ch = dw.shape[1]
    nb = ch // LANES
    return pl.pallas_call(
        _conf_conv_kernel,
        grid=(bsz, nb),
        in_specs=[pl.BlockSpec((None, rows, LANES), lambda b, j: (b, 0, j)),
                  pl.BlockSpec((None, rows, LANES), lambda b, j: (b, 0, nb + j)),
                  pl.BlockSpec((CONF_K, LANES), lambda b, j: (0, j)),
                  pl.BlockSpec((1, LANES), lambda b, j: (0, j))],
        out_specs=pl.BlockSpec((None, rows, LANES), lambda b, j: (b, 0, j)),
        out_shape=jax.ShapeDtypeStruct((bsz, rows, ch), F32),
        scratch_shapes=[pltpu.VMEM((rows + 2 * CONF_PAD, LANES), F32)],
        compiler_params=_params("parallel", "parallel"),
        name="conf_conv",
    )(p, p, dw, dw_b.reshape(1, ch))


def _ln_silu_kernel(x_ref, g_ref, b_ref, o_ref):
    x = x_ref[...]
    xc = x - jnp.mean(x, axis=-1, keepdims=True)
    y = xc * lax.rsqrt(jnp.mean(xc * xc, axis=-1, keepdims=True) + 1e-5) * g_ref[...] + b_ref[...]
    o_ref[...] = _silu(y).astype(o_ref.dtype)


def ln_silu(x, g, b):
    bsz, rows, ch = x.shape
    row = pl.BlockSpec((None, ROW_TILE, ch), lambda bb, t: (bb, t, 0))
    vec = pl.BlockSpec((1, ch), lambda bb, t: (0, 0))
    return pl.pallas_call(
        _ln_silu_kernel,
        grid=(bsz, rows // ROW_TILE),
        in_specs=[row, vec, vec],
        out_specs=row,
        out_shape=jax.ShapeDtypeStruct((bsz, rows, ch), BF16),
        compiler_params=_params("parallel", "parallel"),
        name="ln_silu",
    )(x, g.reshape(1, ch), b.reshape(1, ch))


def _short_conv_kernel(gb_ref, gc_ref, sh_ref, w_ref, o_ref):
    m = gc_ref[...].astype(F32) * sh_ref[...].astype(F32)
    rows = m.shape[0]
    row = lax.broadcasted_iota(jnp.int32, m.shape, 0)
    prev = jnp.where(row == 0, 0.0, pltpu.roll(m, 1, 0))
    nxt = jnp.where(row == rows - 1, 0.0, pltpu.roll(m, rows - 1, 0))
    w = w_ref[...]
    o_ref[...] = (gb_ref[...].astype(F32) * (prev * w[0:1] + m * w[1:2] + nxt * w[2:3])).astype(o_ref.dtype)


def short_conv(p, first_block, w):
    bsz, rows, _ = p.shape
    ch = w.shape[1]
    nb = ch // LANES
    blk = lambda off: pl.BlockSpec((None, rows, LANES), lambda b, j: (b, 0, first_block + off + j))
    return pl.pallas_call(
        _short_conv_kernel,
        grid=(bsz, nb),
        in_specs=[blk(0), blk(nb), blk(2 * nb), pl.BlockSpec((w.shape[0], LANES), lambda b, j: (0, j))],
        out_specs=pl.BlockSpec((None, rows, LANES), lambda b, j: (b, 0, j)),
        out_shape=jax.ShapeDtypeStruct((bsz, rows, ch), BF16),
        compiler_params=_params("parallel", "parallel"),
        name="short_conv",
    )(p, p, p, w)


def _ffn(xn, w_gate, w_up, w_down, layer):
    pad = -w_gate.shape[2] % FF_ALIGN
    wg = cast_weight(w_gate, layer, pad_cols=pad)
    wu = cast_weight(w_up, layer, pad_cols=pad)
    wd = cast_weight(w_down, layer, pad_rows=pad)
    act = ff_up(xn, wg, wu)
    return matmul_ksplit(act, wd, k_steps=4)


def _mixer_even(xn_all, n_ctx, seq, w_in_all, w_out_all, conv_w, a_log, dt_bias, gdn_norm, lam_vecs, diff_norm,
                lambda_init):
    bsz, rows, d = xn_all.shape
    qk_w = GDN_HEADS * GDN_D
    c_ab = 3 * qk_w + qk_w
    n_ab = 4 * GDN_HEADS
    w_in = w_in_all[0].astype(BF16)
    w_ab = jnp.pad(w_in[:, c_ab:c_ab + n_ab], ((0, 0), (0, LANES - n_ab)))
    x2 = xn_all.reshape(bsz * rows, d)
    pa = matmul(x2, w_in, BF16, n_cols=c_ab).reshape(bsz, rows, -1)
    pb = matmul(x2, w_in[:, c_ab + n_ab:], BF16).reshape(bsz, rows, -1)
    ab = matmul(x2, w_ab, F32, tn=LANES).reshape(bsz, rows, LANES)
    qkv = gdn_prep(pa, conv_w, n_ctx)
    gates = gdn_gates(ab, a_log, dt_bias)
    o_f, o_b, c_f, c_b = gdn_scan(qkv, gates, n_ctx)
    z_block = c_ab // (DIFF_HEADS * 2 * DIFF_D) - 1
    ya = gdn_out(o_f, o_b, pa, z_block, n_ctx // ROW_TILE, gdn_norm)
    qr, kr, qc = rope_qk(pb, n_ctx, seq)
    yb = diff_attention(qr, kr, pb, 2 * DIFF_HEADS, lam_vecs, diff_norm, lambda_init)
    w_out = cast_weight(w_out_all, 0)
    y = matmul_cat(ya.reshape(bsz * seq, -1), yb.reshape(bsz * seq, -1), w_out)
    ya_c = gdn_out(c_f, c_b, pa, z_block, 0, gdn_norm)
    yb_c = diff_attention(qc, kr, pb, 2 * DIFF_HEADS, lam_vecs, diff_norm, lambda_init, kv_rows=n_ctx)
    y_c = matmul_cat(ya_c.reshape(bsz * n_ctx, -1), yb_c.reshape(bsz * n_ctx, -1), w_out)
    return y.reshape(bsz, seq, d), y_c.reshape(bsz, n_ctx, d)


def _mixer_odd(xn, w_in_all, w_out_all, conf_dw, conf_dw_b, conf_ln_g, conf_ln_b, sc_w):
    bsz, seq, d = xn.shape
    p = matmul(xn.reshape(bsz * seq, d), cast_weight(w_in_all, 0), BF16).reshape(bsz, seq, -1)
    yc = ln_silu(conf_conv(p, conf_dw, conf_dw_b), conf_ln_g, conf_ln_b)
    yd = short_conv(p, 2 * (conf_dw.shape[1] // LANES), sc_w)
    y = matmul_cat(yc.reshape(bsz * seq, -1), yd.reshape(bsz * seq, -1), cast_weight(w_out_all, 0))
    return y.reshape(bsz, seq, d)


@jax.jit
def _forward(x, c, ctx, c_ctx, w_ada, b_ada, g_mix_pre, g_mix_post, g_ffn_pre, g_ffn_post,
             w_ff_gate, w_ff_up, w_ff_down, w_in_even, w_out_even, gdn_conv, gdn_a_log,
             gdn_dt_bias, gdn_norm, diff_lambda, diff_norm, w_in_odd, w_out_odd, conf_dw,
             conf_dw_b, conf_ln_g, conf_ln_b, sc_conv):
    bsz, seq, d = x.shape
    n_ctx = ctx.shape[1]
    depth = w_ada.shape[0]
    assert depth == 2, "the context stream is only carried through the first (even) layer"
    cond = jnp.concatenate([c, c_ctx[None, :], jnp.zeros((SUBLANES - bsz - 1, d), F32)], axis=0)
    mods = [adaln(cond, w_ada, b_ada, i).reshape(SUBLANES, 6, 1, d) for i in range(depth)]

    xn_all = prenorm_with_ctx(x, ctx, g_mix_pre[0], mods[0], bsz)
    y, y_c = _mixer_even(xn_all, n_ctx, seq, w_in_even, w_out_even, gdn_conv[0], gdn_a_log[0],
                         gdn_dt_bias[0], gdn_norm[0], diff_lambda[0], diff_norm[0], 0.8 - 0.6 * math.exp(-0.3 * 0))
    x, xn = postnorm(x, y, mods[0], 2, g_mix_post[0], (g_ffn_pre[0], mods[0], 3, 4))
    f = _ffn(xn.reshape(bsz * seq, d), w_ff_gate, w_ff_up, w_ff_down, 0).reshape(bsz, seq, d)
    x, xn = postnorm(x, f, mods[0], 5, g_ffn_post[0], (g_mix_pre[1], mods[1], 0, 1))
    h, hn = postnorm(ctx, y_c, mods[0], 2, g_mix_post[0], (g_ffn_pre[0], mods[0], 3, 4), mod_row=bsz)
    f_c = _ffn(hn.reshape(bsz * n_ctx, d), w_ff_gate, w_ff_up, w_ff_down, 0).reshape(bsz, n_ctx, d)
    h = postnorm(h, f_c, mods[0], 5, g_ffn_post[0], mod_row=bsz)
    del h

    y = _mixer_odd(xn, w_in_odd, w_out_odd, conf_dw[0], conf_dw_b[0], conf_ln_g[0], conf_ln_b[0],
                   sc_conv[0])
    x, xn = postnorm(x, y, mods[1], 2, g_mix_post[1], (g_ffn_pre[1], mods[1], 3, 4))
    f = _ffn(xn.reshape(bsz * seq, d), w_ff_gate, w_ff_up, w_ff_down, 1).reshape(bsz, seq, d)
    return postnorm(x, f, mods[1], 5, g_ffn_post[1])


def kernel(x, c, ctx, c_ctx, w_ada, b_ada, g_mix_pre, g_mix_post, g_ffn_pre, g_ffn_post, w_ff_gate, w_ff_up,
           w_ff_down, w_in_even, w_out_even, gdn_conv, gdn_a_log, gdn_dt_bias, gdn_norm, diff_lambda, diff_norm,
           w_in_odd, w_out_odd, conf_dw, conf_dw_b, conf_ln_g, conf_ln_b, sc_conv):
    return _forward(x, c, ctx, c_ctx, w_ada, b_ada, g_mix_pre, g_mix_post, g_ffn_pre, g_ffn_post, w_ff_gate,
                    w_ff_up, w_ff_down, w_in_even, w_out_even, gdn_conv, gdn_a_log, gdn_dt_bias, gdn_norm,
                    diff_lambda, diff_norm, w_in_odd, w_out_odd, conf_dw, conf_dw_b, conf_ln_g, conf_ln_b,
                    sc_conv)
```
